```python
import jax, jax.numpy as jnp
from jax import lax
import numpy as np

D_MODEL = 2048
BATCH = 8
SEQ = 2048
DEPTH = 1
DEC_BATCH = 16
DEC_SEQ = 32
PAST_LEN = 2048

CHUNK = 64
Q_BLOCK = 128
SB_HEAD_DIM = 128
SB_WIDTH = D_MODEL // 2
SB_HEADS = SB_WIDTH // SB_HEAD_DIM
RW_HEAD_DIM = 64
RW_WIDTH = D_MODEL // 2
RW_HEADS = RW_WIDTH // RW_HEAD_DIM
RW_DECAY_RANK = 64
RW_A_RANK = 64
RW_GATE_RANK = 128
RW_PROJ = 3 * RW_WIDTH + RW_DECAY_RANK + RW_A_RANK + RW_GATE_RANK
RW_SPLITS = [RW_WIDTH, 2 * RW_WIDTH, 3 * RW_WIDTH, 3 * RW_WIDTH + RW_DECAY_RANK,
             3 * RW_WIDTH + RW_DECAY_RANK + RW_A_RANK]
N_BRANCH = 2
IN_PROJ = 3 * SB_WIDTH + RW_PROJ + N_BRANCH * D_MODEL
IN_SPLITS = [SB_WIDTH, 2 * SB_WIDTH, 3 * SB_WIDTH, 3 * SB_WIDTH + RW_PROJ,
             3 * SB_WIDTH + RW_PROJ + D_MODEL]
PEER_HEADS = 8
PEER_NKEYS = 128
PEER_EXPERTS = PEER_NKEYS * PEER_NKEYS
PEER_QDIM = 256
PEER_HALF = PEER_QDIM // 2
PEER_TOPK = 16
PEER_TOKEN_BLOCK = 128
RMS_EPS = 1e-6
GN_EPS = 64e-5

kernel_name = 'hybrid_stickbreak_rwkv7_peer_stream_step'


def rms_norm(x, g):
    xf = x.astype(jnp.float32)
    y = xf * lax.rsqrt(jnp.mean(xf * xf, axis=-1, keepdims=True) + RMS_EPS)
    return (y * g.astype(jnp.float32)).astype(x.dtype)


def sb_block(q, k, v, q_pos, k_pos):
    z = jnp.einsum('bqhd,bkhd->bhqk', q, k, preferred_element_type=jnp.float32) * (SB_HEAD_DIM ** -0.5)
    mask = (k_pos[None, :] < q_pos[:, None])[None, None]
    log_keep = jnp.where(mask, jax.nn.log_sigmoid(-z), 0.0)
    log_after = lax.cumsum(log_keep, axis=3, reverse=True) - log_keep
    w = jnp.where(mask, jnp.exp(jax.nn.log_sigmoid(z) + log_after), 0.0)
    return jnp.einsum('bhqk,bkhd->bqhd', w.astype(v.dtype), v)


def sb_attention_prompt(q, k, v):
    B, S, H, dh = q.shape
    nb = S // Q_BLOCK
    pos = jnp.arange(S)
    qb = jnp.moveaxis(q.reshape(B, nb, Q_BLOCK, H, dh), 1, 0)
    posb = pos.reshape(nb, Q_BLOCK)
    out = lax.map(lambda a: sb_block(a[0], k, v, a[1], pos), (qb, posb))
    return jnp.moveaxis(out, 0, 1).reshape(B, S, H, dh)


def rwkv_step(S, inp):
    r_t, w_t, k_t, v_t, kk_t, a_t = inp
    sa = jnp.einsum('bhvk,bhk->bhv', S, kk_t)
    S = (S * w_t[:, :, None, :] - sa[..., None] * (kk_t * a_t)[:, :, None, :]
         + v_t[..., None] * k_t[:, :, None, :])
    y = jnp.einsum('bhvk,bhk->bhv', S, r_t)
    return S, y


def rwkv7_branch(p, shift0, wkv0, mu, w0, w2, a0, a2, g2, k_k, k_a, r_k, ln_g, ln_b):
    B, T, _ = p.shape
    prev = jnp.concatenate([shift0.astype(p.dtype), p[:, :-1]], axis=1)
    ps = p + (prev - p) * mu
    r, k, v, xw, xa, xg = jnp.split(ps, RW_SPLITS, axis=-1)
    f32 = jnp.float32
    w = -jax.nn.softplus(-(w0 + jnp.tanh(xw) @ w2).astype(f32)) - 0.5
    decay = jnp.exp(-jnp.exp(w))
    a = jax.nn.sigmoid((a0 + xa @ a2).astype(f32))
    g = jax.nn.sigmoid(xg) @ g2
    k = k.astype(f32)
    hd = lambda t: t.reshape(B, T, RW_HEADS, RW_HEAD_DIM)
    kk = hd(k * k_k.astype(f32))
    kk = kk / jnp.maximum(jnp.sqrt(jnp.sum(kk * kk, axis=-1, keepdims=True)), 1e-12)
    k = hd(k * (1.0 + (a - 1.0) * k_a.astype(f32)))
    r_h, v_h, a_h, w_h = hd(r.astype(f32)), hd(v.astype(f32)), hd(a), hd(decay)
    tm = lambda t: jnp.moveaxis(t, 1, 0)
    S_final, y = lax.scan(rwkv_step, wkv0.astype(f32), (tm(r_h), tm(w_h), tm(k), tm(v_h), tm(kk), tm(a_h)))
    y = jnp.moveaxis(y, 0, 1)
    mean = jnp.mean(y, axis=-1, keepdims=True)
    var = jnp.mean(jnp.square(y - mean), axis=-1, keepdims=True)
    y = ((y - mean) * lax.rsqrt(var + GN_EPS)).reshape(B, T, RW_WIDTH) * ln_g + ln_b
    bonus = jnp.sum(r_h * k * r_k.astype(f32), axis=-1, keepdims=True) * v_h
    out = ((y + bonus.reshape(B, T, RW_WIDTH)) * g).astype(p.dtype)
    return out, S_final, p[:, -1:]


def peer_ffn(xn, w_q, sub_k1, sub_k2, u_tab, v_tab):
    B, T, D = xn.shape
    xf = xn.reshape(-1, D)
    n = xf.shape[0]
    q = (xf @ w_q).reshape(n, PEER_HEADS, 2, PEER_HALF)
    s1 = jnp.einsum('nhd,kd->nhk', q[:, :, 0], sub_k1, preferred_element_type=jnp.float32)
    s2 = jnp.einsum('nhd,kd->nhk', q[:, :, 1], sub_k2, preferred_element_type=jnp.float32)
    t1, i1 = lax.top_k(s1, PEER_TOPK)
    t2, i2 = lax.top_k(s2, PEER_TOPK)
    cand = (t1[..., :, None] + t2[..., None, :]).reshape(n, PEER_HEADS, PEER_TOPK * PEER_TOPK)
    cidx = (i1[..., :, None] * PEER_NKEYS + i2[..., None, :]).reshape(n, PEER_HEADS, PEER_TOPK * PEER_TOPK)
    best, sel = lax.top_k(cand, PEER_TOPK)
    idx = jnp.take_along_axis(cidx, sel, axis=-1).reshape(n, PEER_HEADS * PEER_TOPK)
    gate = jax.nn.softmax(best, axis=-1).reshape(n, PEER_HEADS * PEER_TOPK).astype(xn.dtype)
    nb = -(-n // PEER_TOKEN_BLOCK)
    pad = nb * PEER_TOKEN_BLOCK - n
    blk = lambda t: jnp.pad(t, ((0, pad), (0, 0))).reshape(nb, PEER_TOKEN_BLOCK, t.shape[-1])

    def one_block(args):
        xb, ib, gb = args
        u = jnp.take(u_tab, ib, axis=0)
        h = jax.nn.gelu(jnp.einsum('nkd,nd->nk', u, xb), approximate=False)
        vv = jnp.take(v_tab, ib, axis=0)
        return jnp.einsum('nk,nkd->nd', gb * h, vv)

    out = lax.map(one_block, (blk(xf), blk(idx), blk(gate)))
    return out.reshape(nb * PEER_TOKEN_BLOCK, D)[:n].reshape(B, T, D)


def layer_forward(x, past_k, past_v, wkv0, shift0, norm1_g, w_in, rw_mu, rw_w0, rw_w2, rw_a0, rw_a2,
                  rw_g2, rw_k_k, rw_k_a, rw_r_k, rw_ln_g, rw_ln_b, w_branch_sb, w_branch_rw, w_out,
                  norm2_g, peer_wq, peer_k1, peer_k2, peer_u, peer_v):
    B, T, _ = x.shape
    xn = rms_norm(x, norm1_g)
    proj = xn @ w_in
    q, k, v, p_rw, gate_sb, gate_rw = jnp.split(proj, IN_SPLITS, axis=-1)
    hs = lambda t: t.reshape(B, T, SB_HEADS, SB_HEAD_DIM)
    q, k, v = hs(q), hs(k), hs(v)
    if past_k is None:
        o_sb = sb_attention_prompt(q, k, v)
        wkv0 = jnp.zeros((B, RW_HEADS, RW_HEAD_DIM, RW_HEAD_DIM), jnp.float32)
        shift0 = jnp.zeros((B, 1, RW_PROJ), x.dtype)
    else:
        P = past_k.shape[1]
        k_all = jnp.concatenate([past_k.astype(k.dtype), k], axis=1)
        v_all = jnp.concatenate([past_v.astype(v.dtype), v], axis=1)
        o_sb = sb_block(q, k_all, v_all, P + jnp.arange(T), jnp.arange(P + T))
    o_rw, wkv_new, shift_new = rwkv7_branch(p_rw, shift0, wkv0, rw_mu, rw_w0, rw_w2, rw_a0, rw_a2, rw_g2,
                                            rw_k_k, rw_k_a, rw_r_k, rw_ln_g, rw_ln_b)
    mixed = (jax.nn.sigmoid(gate_sb) * (o_sb.reshape(B, T, SB_WIDTH) @ w_branch_sb)
             + jax.nn.sigmoid(gate_rw) * (o_rw @ w_branch_rw))
    h = x + mixed @ w_out
    h = h + peer_ffn(rms_norm(h, norm2_g), peer_wq, peer_k1, peer_k2, peer_u, peer_v)
    return h, k, v, wkv_new.astype(x.dtype), shift_new


def setup_inputs(seed: int = 0) -> dict:
    key = jax.random.key(seed)
    ks = jax.random.split(key, 32)
    f32 = jnp.float32
    nrm = lambda k, shape, s: jax.random.normal(k, shape, f32) * s
    L = DEPTH
    return {
        'x_prompt': nrm(ks[0], (BATCH, SEQ, D_MODEL), 1.0),
        'x_sample': nrm(ks[1], (DEC_BATCH, DEC_SEQ, D_MODEL), 1.0),
        'cache_sb_k': nrm(ks[2], (L, DEC_BATCH, PAST_LEN, SB_HEADS, SB_HEAD_DIM), 1.0),
        'cache_sb_v': nrm(ks[3], (L, DEC_BATCH, PAST_LEN, SB_HEADS, SB_HEAD_DIM), 1.0),
        'state_rwkv_wkv': nrm(ks[4], (L, DEC_BATCH, RW_HEADS, RW_HEAD_DIM, RW_HEAD_DIM), 0.5),
        'state_rwkv_shift': nrm(ks[5], (L, DEC_BATCH, 1, RW_PROJ), 1.0),
        'norm1_g': 1.0 + nrm(ks[6], (L, D_MODEL), 0.02),
        'w_in': nrm(ks[7], (L, D_MODEL, IN_PROJ), D_MODEL ** -0.5),
        'rw_mu': jax.random.uniform(ks[8], (L, RW_PROJ), f32, 0.2, 0.8),
        'rw_w0': jax.random.uniform(ks[9], (L, RW_WIDTH), f32, -6.5, -1.5),
        'rw_w2': nrm(ks[10], (L, RW_DECAY_RANK, RW_WIDTH), 0.5 * RW_DECAY_RANK ** -0.5),
        'rw_a0': nrm(ks[11], (L, RW_WIDTH), 0.1),
        'rw_a2': nrm(ks[12], (L, RW_A_RANK, RW_WIDTH), RW_A_RANK ** -0.5),
        'rw_g2': nrm(ks[13], (L, RW_GATE_RANK, RW_WIDTH), RW_GATE_RANK ** -0.5),
        'rw_k_k': 0.85 + nrm(ks[14], (L, RW_WIDTH), 0.02),
        'rw_k_a': 1.0 + nrm(ks[15], (L, RW_WIDTH), 0.02),
        'rw_r_k': nrm(ks[16], (L, RW_HEADS, RW_HEAD_DIM), 0.1),
        'rw_ln_g': 1.0 + nrm(ks[17], (L, RW_WIDTH), 0.02),
        'rw_ln_b': nrm(ks[18], (L, RW_WIDTH), 0.02),
        'w_branch_sb': nrm(ks[19], (L, SB_WIDTH, D_MODEL), SB_WIDTH ** -0.5),
        'w_branch_rw': nrm(ks[20], (L, RW_WIDTH, D_MODEL), RW_WIDTH ** -0.5),
        'w_out': nrm(ks[21], (L, D_MODEL, D_MODEL), D_MODEL ** -0.5),
        'norm2_g': 1.0 + nrm(ks[22], (L, D_MODEL), 0.02),
        'peer_wq': nrm(ks[23], (L, D_MODEL, PEER_HEADS * PEER_QDIM), D_MODEL ** -0.5),
        'peer_k1': nrm(ks[24], (L, PEER_NKEYS, PEER_HALF), PEER_HALF ** -0.5),
        'peer_k2': nrm(ks[25], (L, PEER_NKEYS, PEER_HALF), PEER_HALF ** -0.5),
        'peer_u': nrm(ks[26], (L, PEER_EXPERTS, D_MODEL), D_MODEL ** -0.5),
        'peer_v': nrm(ks[27], (L, PEER_EXPERTS, D_MODEL), 0.5),
        'final_norm_g': 1.0 + nrm(ks[28], (D_MODEL,), 0.02),
    }


def reference(x_prompt, x_sample, cache_sb_k, cache_sb_v, state_rwkv_wkv, state_rwkv_shift,
              norm1_g, w_in, rw_mu, rw_w0, rw_w2, rw_a0, rw_a2, rw_g2, rw_k_k, rw_k_a, rw_r_k,
              rw_ln_g, rw_ln_b, w_branch_sb, w_branch_rw, w_out, norm2_g, peer_wq, peer_k1, peer_k2,
              peer_u, peer_v, final_norm_g):
    hp, hs_ = x_prompt, x_sample
    kp, vp, wp, sp, ks_, vs_, ws_, ss_ = [], [], [], [], [], [], [], []
    for l in range(DEPTH):
        w = (norm1_g[l], w_in[l], rw_mu[l], rw_w0[l], rw_w2[l], rw_a0[l], rw_a2[l], rw_g2[l], rw_k_k[l],
             rw_k_a[l], rw_r_k[l], rw_ln_g[l], rw_ln_b[l], w_branch_sb[l], w_branch_rw[l], w_out[l],
             norm2_g[l], peer_wq[l], peer_k1[l], peer_k2[l], peer_u[l], peer_v[l])
        hp, k1, v1, wk1, sh1 = layer_forward(hp, None, None, None, None, *w)
        hs_, k2, v2, wk2, sh2 = layer_forward(hs_, cache_sb_k[l], cache_sb_v[l], state_rwkv_wkv[l],
                                              state_rwkv_shift[l], *w)
        kp.append(k1); vp.append(v1); wp.append(wk1); sp.append(sh1)
        ks_.append(k2); vs_.append(v2); ws_.append(wk2); ss_.append(sh2)
    y_prompt = rms_norm(hp, final_norm_g)
    y_sample = rms_norm(hs_, final_norm_g)
    return (y_prompt, y_sample, jnp.stack(kp), jnp.stack(vp), jnp.stack(wp), jnp.stack(sp),
            jnp.stack(ks_), jnp.stack(vs_), jnp.stack(ws_), jnp.stack(ss_))
```

```python
import functools

import jax
import jax.numpy as jnp
from jax import lax
from jax.experimental import pallas as pl
from jax.experimental.pallas import tpu as pltpu

F32 = jnp.float32
BF16 = jnp.bfloat16

RMS_EPS = 1e-6
GN_EPS = 64e-5
SB_HEAD_DIM = 128
RW_HEAD_DIM = 64
RW_DECAY_RANK = 64
RW_A_RANK = 64
RW_GATE_RANK = 128
PEER_HEADS = 8
PEER_NKEYS = 128
PEER_HALF = 128
PEER_TOPK = 16

V7X_LANES = 128
V7X_SUBLANES = 8
V7X_VMEM_LIMIT_BYTES = 60000 * 1024

NEG_BIG = -1e30


def _cparams(semantics, vmem_bytes):
    return pltpu.CompilerParams(dimension_semantics=semantics,
                                vmem_limit_bytes=int(min(V7X_VMEM_LIMIT_BYTES, vmem_bytes)))


def _sigmoid(x):
    return 1.0 / (1.0 + jnp.exp(-x))


def _softplus(x):
    return jnp.maximum(x, 0.0) + jnp.log1p(jnp.exp(-jnp.abs(x)))


def _rms(x, g):
    return x * lax.rsqrt(jnp.mean(x * x, axis=-1, keepdims=True) + RMS_EPS) * g


def _dot(a, b):
    return jnp.dot(a, b, preferred_element_type=F32)


def _dot_nt(a, b):
    return lax.dot_general(a, b, (((1,), (1,)), ((), ())), preferred_element_type=F32)


def _dot_tn(a, b):
    return lax.dot_general(a, b, (((0,), (0,)), ((), ())), preferred_element_type=F32)


def _split_bf16(x):
    hi = x.astype(BF16)
    lo = (x - hi.astype(F32)).astype(BF16)
    return hi, lo


def _resident(shape):
    nd = len(shape)
    return pl.BlockSpec(shape, lambda *_: (0,) * nd, pipeline_mode=pl.Buffered(1))


def _norm_matmul_kernel(n_w, x_ref, g_ref, *refs):
    xn = _rms(x_ref[...], g_ref[...]).astype(BF16)
    for w_ref, o_ref in zip(refs[:n_w], refs[n_w:]):
        o_ref[...] = _dot(xn, w_ref[...])


def _norm_matmul(x2d, g, ws, tm=256):
    n, d = x2d.shape
    assert n % tm == 0
    wbytes = sum(w.size * 2 for w in ws)
    obytes = sum(2 * tm * w.shape[1] * 4 for w in ws)
    return pl.pallas_call(
        functools.partial(_norm_matmul_kernel, len(ws)),
        grid=(n // tm,),
        in_specs=[pl.BlockSpec((tm, d), lambda i: (i, 0)), _resident((1, d))] + [_resident(w.shape) for w in ws],
        out_specs=[pl.BlockSpec((tm, w.shape[1]), lambda i: (i, 0)) for w in ws],
        out_shape=[jax.ShapeDtypeStruct((n, w.shape[1]), F32) for w in ws],
        compiler_params=_cparams(("parallel",), wbytes + 2 * obytes + 6 * tm * d * 4 + (8 << 20)),
        name="norm_matmul",
    )(x2d, g.reshape(1, d), *ws)


SB_BLOCK = 128


def _sb_kernel(n_past_static, q_ref, kd_ref, vd_ref, kp_ref, vp_ref, o_ref):
    bq = q_ref.shape[0]
    scale = SB_HEAD_DIM ** -0.5
    q = q_ref[...].astype(BF16)

    def tri(n):
        r = lax.broadcasted_iota(jnp.int32, (n, n), 0)
        c = lax.broadcasted_iota(jnp.int32, (n, n), 1)
        return jnp.where(r > c, 1.0, 0.0).astype(BF16)

    def block(kb, vb, c, acc, mask):
        z = _dot_nt(q, kb.astype(BF16)) * scale
        sp = _softplus(z)
        lk = -sp
        if mask is not None:
            lk = jnp.where(mask, lk, 0.0)
        hi, lo = _split_bf16(lk)
        t = tri(kb.shape[0])
        after = _dot(hi, t) + _dot(lo, t)
        w = jnp.exp(z - sp + after + c)
        if mask is not None:
            w = jnp.where(mask, w, 0.0)
        acc = acc + _dot(w.astype(BF16), vb.astype(BF16))
        c = c + jnp.sum(lk, axis=1, keepdims=True)
        return c, acc

    bd = kd_ref.shape[0]
    r = lax.broadcasted_iota(jnp.int32, (bq, bd), 0)
    cidx = lax.broadcasted_iota(jnp.int32, (bq, bd), 1)
    c0 = jnp.zeros((bq, 1), F32)
    acc0 = jnp.zeros((bq, SB_HEAD_DIM), F32)
    c, acc = block(kd_ref[...], vd_ref[...], c0, acc0, cidx < r)

    n_past = pl.program_id(2) if n_past_static is None else n_past_static

    def body(it, carry):
        start = pl.multiple_of((n_past - 1 - it) * SB_BLOCK, SB_BLOCK)
        return block(kp_ref[pl.ds(start, SB_BLOCK), :], vp_ref[pl.ds(start, SB_BLOCK), :], carry[0], carry[1], None)

    c, acc = lax.fori_loop(0, n_past, body, (c, acc))
    o_ref[...] = acc


def _sb_attention(q, kd, vd, kp, vp, prompt):
    b, tq, width = q.shape
    heads = width // SB_HEAD_DIM
    tp = kp.shape[1]
    assert tp % SB_BLOCK == 0
    if prompt:
        bq, n_past = SB_BLOCK, None
    else:
        bq, n_past = tq, tp // SB_BLOCK
    nq = tq // bq
    qspec = pl.BlockSpec((None, bq, SB_HEAD_DIM), lambda bi, h, i: (bi, i, h))
    pspec = pl.BlockSpec((None, tp, SB_HEAD_DIM), lambda bi, h, i: (bi, 0, h))
    return pl.pallas_call(
        functools.partial(_sb_kernel, n_past),
        grid=(b, heads, nq),
        in_specs=[qspec, qspec, qspec, pspec, pspec],
        out_specs=qspec,
        out_shape=jax.ShapeDtypeStruct((b, tq, width), F32),
        compiler_params=_cparams(("parallel", "parallel", "arbitrary"), 4 * tp * SB_HEAD_DIM * 4 + (16 << 20)),
        name="sb_attention",
    )(q, kd, vd, kp, vp)


def _rw_prep_kernel(rw_width, p_ref, sh_ref, mu_ref, w0_ref, a0_ref, wa2_ref, g2_ref,
                    r_o, w_o, k_o, v_o, a_o, g_o, carry):
    @pl.when(pl.program_id(1) == 0)
    def _():
        carry[...] = sh_ref[...]

    p = p_ref[...]
    tt = p.shape[0]
    row = lax.broadcasted_iota(jnp.int32, p.shape, 0)
    prev = jnp.where(row == 0, carry[...], pltpu.roll(p, 1, axis=0))
    carry[...] = p[tt - 1:tt, :]
    ps = p + (prev - p) * mu_ref[...]
    c = rw_width
    r_o[...] = ps[:, 0:c]
    k_o[...] = ps[:, c:2 * c]
    v_o[...] = ps[:, 2 * c:3 * c]
    xwa = ps[:, 3 * c:3 * c + RW_DECAY_RANK + RW_A_RANK]
    lane = lax.broadcasted_iota(jnp.int32, xwa.shape, 1)
    lowrank_in = jnp.where(lane < RW_DECAY_RANK, jnp.tanh(xwa), xwa).astype(BF16)
    wa = _dot(lowrank_in, wa2_ref[...])
    w = -_softplus(-(w0_ref[...] + wa[:, 0:c])) - 0.5
    w_o[...] = jnp.exp(-jnp.exp(w))
    a_o[...] = _sigmoid(a0_ref[...] + wa[:, c:2 * c])
    xg = ps[:, 3 * c + RW_DECAY_RANK + RW_A_RANK:]
    g_o[...] = _dot(_sigmoid(xg).astype(BF16), g2_ref[...])


def _rw_prep(p, shift0, mu, w0, a0, wa2, g2, tt):
    b, t, proj = p.shape
    c = w0.shape[-1]
    assert t % tt == 0
    tok = pl.BlockSpec((None, tt, c), lambda bi, j: (bi, j, 0))
    return pl.pallas_call(
        functools.partial(_rw_prep_kernel, c),
        grid=(b, t // tt),
        in_specs=[pl.BlockSpec((None, tt, proj), lambda bi, j: (bi, j, 0)),
                  pl.BlockSpec((None, 1, proj), lambda bi, j: (bi, 0, 0)),
                  _resident((1, proj)), _resident((1, c)), _resident((1, c)),
                  _resident(wa2.shape), _resident(g2.shape)],
        out_specs=[tok] * 6,
        out_shape=[jax.ShapeDtypeStruct((b, t, c), F32)] * 6,
        scratch_shapes=[pltpu.VMEM((1, proj), F32)],
        compiler_params=_cparams(("parallel", "arbitrary"), 2 * tt * proj * 4 * 4 + 12 * tt * c * 4 + (8 << 20)),
        name="rw_prep",
    )(p, shift0, mu.reshape(1, proj), w0.reshape(1, c), a0.reshape(1, c), wa2, g2)


def _rw_scan_kernel(r_ref, w_ref, k_ref, v_ref, a_ref, kk_p, ka_p, rk_p, lng_p, lnb_p, s0_ref,
                    o_ref, sf_ref, S, kk_s, b_s, kt_s):
    j = pl.program_id(1)
    nk = S.shape[0]
    tt = r_ref.shape[0]

    @pl.when(j == 0)
    def _():
        S[...] = s0_ref[...]

    def step(t, carry):
        r = r_ref[t]
        k = k_ref[t]
        v = v_ref[t]
        a = a_ref[t]
        kk = k * kk_p[...]
        nrm = jnp.sqrt(jnp.sum(kk * kk, axis=0, keepdims=True))
        kk = kk / jnp.maximum(nrm, 1e-12)
        kt = k * (1.0 + (a - 1.0) * ka_p[...])
        kk_s[...] = kk
        b_s[...] = kk * a
        kt_s[...] = kt

        def p1(kx, sa):
            return sa + S[kx] * kk_s[pl.ds(kx, 1), :]

        sa = lax.fori_loop(0, nk, p1, jnp.zeros(v.shape, F32), unroll=8)

        def p2(kx, y):
            row = pl.ds(kx, 1)
            s_new = S[kx] * w_ref[t, row, :] - sa * b_s[row, :] + v * kt_s[row, :]
            S[kx] = s_new
            return y + s_new * r_ref[t, row, :]

        y = lax.fori_loop(0, nk, p2, jnp.zeros(v.shape, F32), unroll=8)
        mean = jnp.mean(y, axis=0, keepdims=True)
        d = y - mean
        var = jnp.mean(d * d, axis=0, keepdims=True)
        bonus = jnp.sum(r * kt * rk_p[...], axis=0, keepdims=True) * v
        o_ref[t] = d * lax.rsqrt(var + GN_EPS) * lng_p[...] + lnb_p[...] + bonus
        return carry

    lax.fori_loop(0, tt, step, 0)

    @pl.when(j == pl.num_programs(1) - 1)
    def _():
        sf_ref[...] = S[...]


def _rw_scan(rT, wT, kT, vT, aT, params, s0, tt):
    t, n, ctot = rT.shape
    assert t % tt == 0 and ctot % V7X_LANES == 0
    seq = pl.BlockSpec((tt, n, V7X_LANES), lambda c, j: (j, 0, c))
    par = pl.BlockSpec((n, V7X_LANES), lambda c, j: (0, c))
    st = pl.BlockSpec((n, n, V7X_LANES), lambda c, j: (0, 0, c))
    return pl.pallas_call(
        _rw_scan_kernel,
        grid=(ctot // V7X_LANES, t // tt),
        in_specs=[seq] * 5 + [par] * 5 + [st],
        out_specs=[seq, st],
        out_shape=[jax.ShapeDtypeStruct((t, n, ctot), F32), jax.ShapeDtypeStruct((n, n, ctot), F32)],
        scratch_shapes=[pltpu.VMEM((n, n, V7X_LANES), F32)] + [pltpu.VMEM((n, V7X_LANES), F32)] * 3,
        compiler_params=_cparams(("parallel", "arbitrary"), 12 * tt * n * V7X_LANES * 4 + 5 * n * n * V7X_LANES * 4 + (8 << 20)),
        name="rw_scan",
    )(rT, wT, kT, vT, aT, *params, s0)


def _mix_kernel(osb_ref, orw_ref, g_ref, gsb_ref, grw_ref, x_ref, wsb_ref, wrw_ref, wout_ref, h_ref):
    sb = _dot(osb_ref[...].astype(BF16), wsb_ref[...])
    rw = _dot((orw_ref[...] * g_ref[...]).astype(BF16), wrw_ref[...])
    mixed = _sigmoid(gsb_ref[...]) * sb + _sigmoid(grw_ref[...]) * rw
    h_ref[...] = x_ref[...] + _dot(mixed.astype(BF16), wout_ref[...])


def _mix(o_sb, o_rw, g, gate_sb, gate_rw, x2d, wsb, wrw, wout, tm=256):
    n, d = x2d.shape
    c = o_sb.shape[1]
    assert n % tm == 0
    half = pl.BlockSpec((tm, c), lambda i: (i, 0))
    full = pl.BlockSpec((tm, d), lambda i: (i, 0))
    return pl.pallas_call(
        _mix_kernel,
        grid=(n // tm,),
        in_specs=[half, half, half, full, full, full, _resident(wsb.shape), _resident(wrw.shape), _resident(wout.shape)],
        out_specs=full,
        out_shape=jax.ShapeDtypeStruct((n, d), F32),
        compiler_params=_cparams(("parallel",), (wsb.size + wrw.size + wout.size) * 2 + 2 * tm * (3 * c + 4 * d) * 4 + (12 << 20)),
        name="branch_mix",
    )(o_sb, o_rw, g, gate_sb, gate_rw, x2d, wsb, wrw, wout)


PEER_RANKS = PEER_TOPK + 1
PEER_RANK_ROWS = 24


def _top_ranks(s, t_ref):
    t_ref[...] = jnp.full(t_ref.shape, NEG_BIG, F32)
    for rnk in range(PEER_RANKS):
        m = jnp.max(s, axis=0, keepdims=True)
        t_ref[rnk:rnk + 1, :] = m
        s = jnp.where(s == m, NEG_BIG, s)


def _peer_select_kernel(h_ref, g_ref, wq_ref, k1_ref, k2_ref, xn_o, thr_o, coef_o, s2_o, e2_o, t1_s, t2_s):
    xn = _rms(h_ref[...], g_ref[...]).astype(BF16)
    xn_o[...] = xn
    q = _dot(xn, wq_ref[...])
    k1h, k1l = _split_bf16(k1_ref[...])
    k2h, k2l = _split_bf16(k2_ref[...])
    sub = V7X_SUBLANES

    def scores(kh, kl, qh_):
        qh, ql = _split_bf16(qh_)
        return _dot_nt(kh, qh) + (_dot_nt(kh, ql) + _dot_nt(kl, qh))

    for h in range(PEER_HEADS):
        base = 2 * PEER_HALF * h
        s1 = scores(k1h, k1l, q[:, base:base + PEER_HALF])
        s2 = scores(k2h, k2l, q[:, base + PEER_HALF:base + 2 * PEER_HALF])
        _top_ranks(s1, t1_s)
        _top_ranks(s2, t2_s)
        t1 = t1_s[...]
        t2 = t2_s[...]
        cand = jnp.concatenate(
            [t1[0:1] + t2[0:sub], t1[0:1] + t2[sub:2 * sub], t1[0:1] + t2[2 * sub:3 * sub],
             t2[0:1] + t1[0:sub], t2[0:1] + t1[sub:2 * sub], t2[0:1] + t1[2 * sub:3 * sub],
             t1[1:2] + t2[0:sub], t2[1:2] + t1[0:sub],
             t1[2:3] + t2[0:sub], t1[3:4] + t2[0:sub], t1[4:5] + t2[0:sub]], axis=0)
        top = jnp.max(cand, axis=0, keepdims=True)
        zsum = jnp.zeros(top.shape, F32)
        lo16 = top
        for rnk in range(PEER_RANKS):
            m = jnp.max(cand, axis=0, keepdims=True)
            if rnk < PEER_TOPK:
                zsum = zsum + jnp.exp(m - top)
                lo16 = m
            else:
                tau = 0.5 * (lo16 + m)
            cand = jnp.where(cand == m, NEG_BIG, cand)
        thr_o[h] = tau - s1
        coef_o[h] = jnp.exp(s1 - t1[0:1]) / zsum
        s2_o[h] = s2
        e2_o[h] = jnp.exp(s2 - t2[0:1])


def _peer_select(h2d, g, wq, k1, k2, tn):
    n, d = h2d.shape
    assert n % tn == 0
    sel = pl.BlockSpec((PEER_HEADS, PEER_NKEYS, tn), lambda i: (0, 0, i))
    sel_shape = jax.ShapeDtypeStruct((PEER_HEADS, PEER_NKEYS, n), F32)
    return pl.pallas_call(
        _peer_select_kernel,
        grid=(n // tn,),
        in_specs=[pl.BlockSpec((tn, d), lambda i: (i, 0)), _resident((1, d)), _resident(wq.shape),
                  _resident(k1.shape), _resident(k2.shape)],
        out_specs=[pl.BlockSpec((tn, d), lambda i: (i, 0)), sel, sel, sel, sel],
        out_shape=[jax.ShapeDtypeStruct((n, d), BF16), sel_shape, sel_shape, sel_shape, sel_shape],
        scratch_shapes=[pltpu.VMEM((PEER_RANK_ROWS, tn), F32)] * 2,
        compiler_params=_cparams(("parallel",), wq.size * 2 + 8 * PEER_HEADS * PEER_NKEYS * tn * 4 + 8 * tn * d * 4 + (12 << 20)),
        name="peer_select",
    )(h2d, g.reshape(1, d), wq, k1, k2)


def _peer_main_kernel(xn_ref, u_ref, v_ref, thr_ref, coef_ref, s2_ref, e2_ref, h_ref, g_ref, y_ref, acc):
    j = pl.program_id(1)
    eb = u_ref.shape[0]
    nib = eb // PEER_NKEYS

    @pl.when(j == 0)
    def _():
        acc[...] = jnp.zeros(acc.shape, F32)

    hid = _dot_nt(u_ref[...], xn_ref[...])
    parts = []
    for ii in range(nib):
        row = pl.ds(j * nib + ii, 1)
        wgt = jnp.zeros((PEER_NKEYS, hid.shape[1]), F32)
        for h in range(PEER_HEADS):
            picked = jnp.where(s2_ref[h] > thr_ref[h, row, :], e2_ref[h], 0.0)
            wgt = wgt + coef_ref[h, row, :] * picked
        x = hid[ii * PEER_NKEYS:(ii + 1) * PEER_NKEYS, :]
        gelu = 0.5 * x * (1.0 + lax.erf(x * (2.0 ** -0.5)))
        parts.append((gelu * wgt).astype(BF16))
    gated = jnp.concatenate(parts, axis=0)
    acc[...] += _dot_tn(gated, v_ref[...])

    @pl.when(j == pl.num_programs(1) - 1)
    def _():
        y_ref[...] = _rms(h_ref[...] + acc[...], g_ref[...])


def _peer_main(xn, u, v, thr, coef, s2, e2, h2d, g, tn, eb):
    n, d = h2d.shape
    e = u.shape[0]
    assert n % tn == 0 and e % eb == 0 and eb % PEER_NKEYS == 0
    sel = pl.BlockSpec((PEER_HEADS, PEER_NKEYS, tn), lambda i, j: (0, 0, i))
    tok = pl.BlockSpec((tn, d), lambda i, j: (i, 0))
    tab = pl.BlockSpec((eb, d), lambda i, j: (j, 0))
    vmem = 2 * tn * d * 2 + 4 * eb * d * 2 + 8 * PEER_HEADS * PEER_NKEYS * tn * 4 + 5 * tn * d * 4 + (8 << 20)
    return pl.pallas_call(
        _peer_main_kernel,
        grid=(n // tn, e // eb),
        in_specs=[tok, tab, tab, sel, sel, sel, sel, tok, _resident((1, d))],
        out_specs=tok,
        out_shape=jax.ShapeDtypeStruct((n, d), F32),
        scratch_shapes=[pltpu.VMEM((tn, d), F32)],
        compiler_params=_cparams(("parallel", "arbitrary"), vmem),
        name="peer_main",
    )(xn, u, v, thr, coef, s2, e2, h2d, g.reshape(1, d))


def _to_chain_major(x, heads):
    b, t, c = x.shape
    n = c // heads
    return x.reshape(b, t, heads, n).transpose(1, 3, 0, 2).reshape(t, n, b * heads)


def _from_chain_major(x, b):
    t, n, ch = x.shape
    heads = ch // b
    return x.reshape(t, n, b, heads).transpose(2, 0, 3, 1).reshape(b, t, heads * n)


def _head_param(p, heads, b):
    n = p.size // heads
    return jnp.tile(p.reshape(heads, n).T, (1, b))


def _layer(x, past_k, past_v, wkv0, shift0, wts, rw_tt, peer_tn, peer_eb):
    b, t, d = x.shape
    n_tok = b * t
    x2d = x.reshape(n_tok, d)
    sbw = wts["wq"].shape[1]
    rww = wts["w0"].shape[-1]
    rw_heads = rww // RW_HEAD_DIM
    proj = wts["w_rw"].shape[1]

    q, k, v = _norm_matmul(x2d, wts["norm1_g"], [wts["wq"], wts["wk"], wts["wv"]])
    (p_rw,) = _norm_matmul(x2d, wts["norm1_g"], [wts["w_rw"]])
    gate_sb, gate_rw = _norm_matmul(x2d, wts["norm1_g"], [wts["w_gsb"], wts["w_grw"]])

    q3, k3, v3 = (a.reshape(b, t, sbw) for a in (q, k, v))
    if past_k is None:
        o_sb = _sb_attention(q3, k3, v3, k3, v3, prompt=True)
        wkv0 = jnp.zeros((b, rw_heads, RW_HEAD_DIM, RW_HEAD_DIM), F32)
        shift0 = jnp.zeros((b, 1, proj), F32)
    else:
        pk = past_k.reshape(b, past_k.shape[1], sbw)
        pv = past_v.reshape(b, past_v.shape[1], sbw)
        o_sb = _sb_attention(q3, k3, v3, pk, pv, prompt=False)

    p3 = p_rw.reshape(b, t, proj)
    r, w, kr, vr, a, g = _rw_prep(p3, shift0, wts["rw_mu"], wts["w0"], wts["a0"], wts["wa2"], wts["g2"], tt=min(t, 256))
    seqs = [_to_chain_major(s, rw_heads) for s in (r, w, kr, vr, a)]
    params = [_head_param(wts[nm], rw_heads, b) for nm in ("k_k", "k_a", "r_k", "ln_g", "ln_b")]
    s0 = wkv0.astype(F32).transpose(3, 2, 0, 1).reshape(RW_HEAD_DIM, RW_HEAD_DIM, b * rw_heads)
    oT, sT = _rw_scan(*seqs, params, s0, tt=rw_tt)
    o_rw = _from_chain_major(oT, b).reshape(n_tok, rww)
    wkv_new = sT.reshape(RW_HEAD_DIM, RW_HEAD_DIM, b, rw_heads).transpose(2, 3, 1, 0)
    shift_new = p3[:, t - 1:t, :]

    h = _mix(o_sb.reshape(n_tok, sbw), o_rw, g.reshape(n_tok, rww), gate_sb, gate_rw, x2d,
             wts["w_bsb"], wts["w_brw"], wts["w_out"])
    xn2, thr, coef, s2, e2 = _peer_select(h, wts["norm2_g"], wts["peer_wq"], wts["peer_k1"], wts["peer_k2"], tn=256)
    y = _peer_main(xn2, wts["peer_u"], wts["peer_v"], thr, coef, s2, e2, h, wts["final_g"], tn=peer_tn, eb=peer_eb)
    heads_sb = sbw // SB_HEAD_DIM
    kv_shape = (b, t, heads_sb, SB_HEAD_DIM)
    return y.reshape(b, t, d), k.reshape(kv_shape), v.reshape(kv_shape), wkv_new, shift_new


def kernel(x_prompt, x_sample, cache_sb_k, cache_sb_v, state_rwkv_wkv, state_rwkv_shift, norm1_g, w_in, rw_mu, rw_w0, rw_w2, rw_a0, rw_a2, rw_g2, rw_k_k, rw_k_a, rw_r_k, rw_ln_g, rw_ln_b, w_branch_sb, w_branch_rw, w_out, norm2_g, peer_wq, peer_k1, peer_k2, peer_u, peer_v, final_norm_g):
    assert norm1_g.shape[0] == 1, "single-layer stack"
    d = x_prompt.shape[-1]
    sbw = w_branch_sb.shape[1]
    rww = w_branch_rw.shape[1]
    proj = rw_mu.shape[1]
    w = w_in[0]
    o = 0
    cols = {}
    for nm, width in (("wq", sbw), ("wk", sbw), ("wv", sbw), ("w_rw", proj), ("w_gsb", d), ("w_grw", d)):
        cols[nm] = w[:, o:o + width].astype(BF16)
        o += width
    zeros = jnp.zeros((RW_DECAY_RANK, rww), F32)
    wa2 = jnp.concatenate([jnp.concatenate([rw_w2[0], zeros], axis=1),
                           jnp.concatenate([zeros, rw_a2[0]], axis=1)], axis=0).astype(BF16)
    wts = dict(cols, norm1_g=norm1_g[0], rw_mu=rw_mu[0], w0=rw_w0[0], a0=rw_a0[0], wa2=wa2, g2=rw_g2[0].astype(BF16),
               k_k=rw_k_k[0], k_a=rw_k_a[0], r_k=rw_r_k[0].reshape(-1), ln_g=rw_ln_g[0], ln_b=rw_ln_b[0],
               w_bsb=w_branch_sb[0].astype(BF16), w_brw=w_branch_rw[0].astype(BF16), w_out=w_out[0].astype(BF16),
               norm2_g=norm2_g[0], peer_wq=peer_wq[0].astype(BF16), peer_k1=peer_k1[0], peer_k2=peer_k2[0],
               peer_u=peer_u[0].astype(BF16), peer_v=peer_v[0].astype(BF16), final_g=final_norm_g)

    yp, kp, vp, wp, sp = _layer(x_prompt, None, None, None, None, wts, rw_tt=32, peer_tn=512, peer_eb=512)
    ys, ks, vs, ws, ss = _layer(x_sample, cache_sb_k[0], cache_sb_v[0], state_rwkv_wkv[0], state_rwkv_shift[0], wts,
                                rw_tt=32, peer_tn=512, peer_eb=512)
    st = lambda t: t[None]
    return (yp, ys, st(kp), st(vp), st(wp.astype(x_prompt.dtype)), st(sp), st(ks), st(vs), st(ws.astype(x_sample.dtype)), st(ss))
```

```python
import functools

import jax
import jax.numpy as jnp
from jax import lax
from jax.experimental import pallas as pl
from jax.experimental.pallas import tpu as pltpu

F32 = jnp.float32
BF16 = jnp.bfloat16

RMS_EPS = 1e-6
GN_EPS = 64e-5
SB_HEAD_DIM = 128
RW_HEAD_DIM = 64
RW_DECAY_RANK = 64
RW_A_RANK = 64
RW_GATE_RANK = 128
PEER_HEADS = 8
PEER_NKEYS = 128
PEER_HALF = 128
PEER_TOPK = 16

V7X_LANES = 128
V7X_SUBLANES = 8
V7X_VMEM_LIMIT_BYTES = 60000 * 1024

NEG_BIG = -1e30


def _cparams(semantics, vmem_bytes):
    return pltpu.CompilerParams(dimension_semantics=semantics,
                                vmem_limit_bytes=int(min(V7X_VMEM_LIMIT_BYTES, vmem_bytes)))


def _sigmoid(x):
    return 1.0 / (1.0 + jnp.exp(-x))


def _softplus(x):
    return jnp.maximum(x, 0.0) + jnp.log1p(jnp.exp(-jnp.abs(x)))


def _rms(x, g):
    return x * lax.rsqrt(jnp.mean(x * x, axis=-1, keepdims=True) + RMS_EPS) * g


def _dot(a, b):
    return jnp.dot(a, b, preferred_element_type=F32)


def _dot_nt(a, b):
    return lax.dot_general(a, b, (((1,), (1,)), ((), ())), preferred_element_type=F32)


def _dot_tn(a, b):
    return lax.dot_general(a, b, (((0,), (0,)), ((), ())), preferred_element_type=F32)


def _split_bf16(x):
    hi = x.astype(BF16)
    lo = (x - hi.astype(F32)).astype(BF16)
    return hi, lo


def _resident(shape):
    nd = len(shape)
    return pl.BlockSpec(shape, lambda *_: (0,) * nd, pipeline_mode=pl.Buffered(1))


def _norm_matmul_kernel(n_w, x_ref, g_ref, *refs):
    xn = _rms(x_ref[...], g_ref[...]).astype(BF16)
    for w_ref, o_ref in zip(refs[:n_w], refs[n_w:]):
        o_ref[...] = _dot(xn, w_ref[...])


def _norm_matmul(x2d, g, ws, tm=256):
    n, d = x2d.shape
    assert n % tm == 0
    wbytes = sum(w.size * 2 for w in ws)
    obytes = sum(2 * tm * w.shape[1] * 4 for w in ws)
    return pl.pallas_call(
        functools.partial(_norm_matmul_kernel, len(ws)),
        grid=(n // tm,),
        in_specs=[pl.BlockSpec((tm, d), lambda i: (i, 0)), _resident((1, d))] + [_resident(w.shape) for w in ws],
        out_specs=[pl.BlockSpec((tm, w.shape[1]), lambda i: (i, 0)) for w in ws],
        out_shape=[jax.ShapeDtypeStruct((n, w.shape[1]), F32) for w in ws],
        compiler_params=_cparams(("parallel",), wbytes + 2 * obytes + 6 * tm * d * 4 + (8 << 20)),
        name="norm_matmul",
    )(x2d, g.reshape(1, d), *ws)


SB_SUPER = 512
SB_SUB = 256
LOG2E = 1.4426950408889634


def _sb_kernel(n_past_static, hp, q_ref, kd_ref, vd_ref, kp_ref, vp_ref, o_ref, acc_s, q_s):
    bq = q_ref.shape[0]
    dh = SB_HEAD_DIM
    q_s[...] = (q_ref[...] * (dh ** -0.5 * LOG2E)).astype(BF16)

    def neg_tri(n):
        r = lax.broadcasted_iota(jnp.int32, (n, n), 0)
        c = lax.broadcasted_iota(jnp.int32, (n, n), 1)
        return jnp.where(r > c, -1.0, 0.0).astype(BF16)

    def superblock(h, kb, vb, c, masked, first):
        cols = slice(h * dh, (h + 1) * dh)
        ks = kb.shape[0]
        sub = min(SB_SUB, ks)
        nsub = ks // sub
        z = _dot_nt(q_s[:, cols], kb.astype(BF16))
        sp = jnp.maximum(z, 0.0) + jnp.log2(1.0 + jnp.exp2(-jnp.abs(z)))
        if masked:
            allowed = lax.broadcasted_iota(jnp.int32, (bq, ks), 1) < lax.broadcasted_iota(jnp.int32, (bq, ks), 0)
            sp = jnp.where(allowed, sp, 0.0)
        sp16 = sp.astype(BF16)
        stacked = jnp.concatenate([sp16[:, k * sub:(k + 1) * sub] for k in range(nsub)], axis=0)
        after = _dot(stacked, neg_tri(sub))
        ws = [None] * nsub
        for k in reversed(range(nsub)):
            kc = slice(k * sub, (k + 1) * sub)
            w = jnp.exp2(z[:, kc] - sp[:, kc] + after[k * bq:(k + 1) * bq] + c)
            if masked:
                w = jnp.where(allowed[:, kc], w, 0.0)
            ws[k] = w.astype(BF16)
            c = c - jnp.sum(sp[:, kc], axis=1, keepdims=True)
        pv = _dot(jnp.concatenate(ws, axis=1), vb.astype(BF16))
        if first:
            acc_s[:, cols] = pv
        else:
            acc_s[:, cols] += pv
        return c

    cs = tuple(superblock(h, kd_ref[:, h * dh:(h + 1) * dh], vd_ref[:, h * dh:(h + 1) * dh],
                          jnp.zeros((bq, 1), F32), True, True) for h in range(hp))

    n_past = pl.program_id(2) if n_past_static is None else n_past_static

    def body(it, cs):
        rows = pl.ds(pl.multiple_of((n_past - 1 - it) * SB_SUPER, SB_SUPER), SB_SUPER)
        return tuple(superblock(h, kp_ref[rows, h * dh:(h + 1) * dh], vp_ref[rows, h * dh:(h + 1) * dh],
                                cs[h], False, False) for h in range(hp))

    lax.fori_loop(0, n_past, body, cs)
    o_ref[...] = acc_s[...]


def _sb_attention(q, kd, vd, kp, vp, prompt, hp):
    b, tq, width = q.shape
    heads = width // SB_HEAD_DIM
    tp = kp.shape[1]
    assert tp % SB_SUPER == 0 and heads % hp == 0
    if prompt:
        bq, n_past = SB_SUPER, None
    else:
        bq, n_past = tq, tp // SB_SUPER
    assert tq % bq == 0
    gw = hp * SB_HEAD_DIM
    qspec = pl.BlockSpec((None, bq, gw), lambda bi, h, i: (bi, i, h))
    pspec = pl.BlockSpec((None, tp, gw), lambda bi, h, i: (bi, 0, h))
    return pl.pallas_call(
        functools.partial(_sb_kernel, n_past, hp),
        grid=(b, heads // hp, tq // bq),
        in_specs=[qspec, qspec, qspec, pspec, pspec],
        out_specs=qspec,
        out_shape=jax.ShapeDtypeStruct((b, tq, width), F32),
        scratch_shapes=[pltpu.VMEM((bq, gw), F32), pltpu.VMEM((bq, gw), BF16)],
        compiler_params=_cparams(("parallel", "parallel", "arbitrary"),
                                 4 * tp * gw * 4 + 10 * bq * gw * 4 + 12 * bq * SB_SUPER * 4 * hp + (8 << 20)),
        name="sb_attention",
    )(q, kd, vd, kp, vp)


def _rw_prep_kernel(rw_width, p_ref, sh_ref, mu_ref, w0_ref, a0_ref, wa2_ref, g2_ref,
                    r_o, w_o, k_o, v_o, a_o, g_o, carry):
    @pl.when(pl.program_id(1) == 0)
    def _():
        carry[...] = sh_ref[...]

    p = p_ref[...]
    tt = p.shape[0]
    row = lax.broadcasted_iota(jnp.int32, p.shape, 0)
    prev = jnp.where(row == 0, carry[...], pltpu.roll(p, 1, axis=0))
    carry[...] = p[tt - 1:tt, :]
    ps = p + (prev - p) * mu_ref[...]
    c = rw_width
    r_o[...] = ps[:, 0:c]
    k_o[...] = ps[:, c:2 * c]
    v_o[...] = ps[:, 2 * c:3 * c]
    xwa = ps[:, 3 * c:3 * c + RW_DECAY_RANK + RW_A_RANK]
    lane = lax.broadcasted_iota(jnp.int32, xwa.shape, 1)
    lowrank_in = jnp.where(lane < RW_DECAY_RANK, jnp.tanh(xwa), xwa).astype(BF16)
    wa = _dot(lowrank_in, wa2_ref[...])
    w = -_softplus(-(w0_ref[...] + wa[:, 0:c])) - 0.5
    w_o[...] = jnp.exp(-jnp.exp(w))
    a_o[...] = _sigmoid(a0_ref[...] + wa[:, c:2 * c])
    xg = ps[:, 3 * c + RW_DECAY_RANK + RW_A_RANK:]
    g_o[...] = _dot(_sigmoid(xg).astype(BF16), g2_ref[...])


def _rw_prep(p, shift0, mu, w0, a0, wa2, g2, tt):
    b, t, proj = p.shape
    c = w0.shape[-1]
    assert t % tt == 0
    tok = pl.BlockSpec((None, tt, c), lambda bi, j: (bi, j, 0))
    return pl.pallas_call(
        functools.partial(_rw_prep_kernel, c),
        grid=(b, t // tt),
        in_specs=[pl.BlockSpec((None, tt, proj), lambda bi, j: (bi, j, 0)),
                  pl.BlockSpec((None, 1, proj), lambda bi, j: (bi, 0, 0)),
                  _resident((1, proj)), _resident((1, c)), _resident((1, c)),
                  _resident(wa2.shape), _resident(g2.shape)],
        out_specs=[tok] * 6,
        out_shape=[jax.ShapeDtypeStruct((b, t, c), F32)] * 6,
        scratch_shapes=[pltpu.VMEM((1, proj), F32)],
        compiler_params=_cparams(("parallel", "arbitrary"), 2 * tt * proj * 4 * 4 + 12 * tt * c * 4 + (8 << 20)),
        name="rw_prep",
    )(p, shift0, mu.reshape(1, proj), w0.reshape(1, c), a0.reshape(1, c), wa2, g2)


def _rw_scan_kernel(r_ref, w_ref, k_ref, v_ref, a_ref, kk_p, ka_p, rk_p, lng_p, lnb_p, s0_ref,
                    o_ref, sf_ref, S, kk_s, b_s, kt_s):
    j = pl.program_id(1)
    nk = S.shape[0]
    tt = r_ref.shape[0]

    @pl.when(j == 0)
    def _():
        S[...] = s0_ref[...]

    def step(t, carry):
        r = r_ref[t]
        k = k_ref[t]
        v = v_ref[t]
        a = a_ref[t]
        kk = k * kk_p[...]
        nrm = jnp.sqrt(jnp.sum(kk * kk, axis=0, keepdims=True))
        kk = kk / jnp.maximum(nrm, 1e-12)
        kt = k * (1.0 + (a - 1.0) * ka_p[...])
        kk_s[...] = kk
        b_s[...] = kk * a
        kt_s[...] = kt

        def p1(kx, sa):
            return sa + S[kx] * kk_s[pl.ds(kx, 1), :]

        sa = lax.fori_loop(0, nk, p1, jnp.zeros(v.shape, F32), unroll=8)

        def p2(kx, y):
            row = pl.ds(kx, 1)
            s_new = S[kx] * w_ref[t, row, :] - sa * b_s[row, :] + v * kt_s[row, :]
            S[kx] = s_new
            return y + s_new * r_ref[t, row, :]

        y = lax.fori_loop(0, nk, p2, jnp.zeros(v.shape, F32), unroll=8)
        mean = jnp.mean(y, axis=0, keepdims=True)
        d = y - mean
        var = jnp.mean(d * d, axis=0, keepdims=True)
        bonus = jnp.sum(r * kt * rk_p[...], axis=0, keepdims=True) * v
        o_ref[t] = d * lax.rsqrt(var + GN_EPS) * lng_p[...] + lnb_p[...] + bonus
        return carry

    lax.fori_loop(0, tt, step, 0)

    @pl.when(j == pl.num_programs(1) - 1)
    def _():
        sf_ref[...] = S[...]


def _rw_scan(rT, wT, kT, vT, aT, params, s0, tt):
    t, n, ctot = rT.shape
    assert t % tt == 0 and ctot % V7X_LANES == 0
    seq = pl.BlockSpec((tt, n, V7X_LANES), lambda c, j: (j, 0, c))
    par = pl.BlockSpec((n, V7X_LANES), lambda c, j: (0, c))
    st = pl.BlockSpec((n, n, V7X_LANES), lambda c, j: (0, 0, c))
    return pl.pallas_call(
        _rw_scan_kernel,
        grid=(ctot // V7X_LANES, t // tt),
        in_specs=[seq] * 5 + [par] * 5 + [st],
        out_specs=[seq, st],
        out_shape=[jax.ShapeDtypeStruct((t, n, ctot), F32), jax.ShapeDtypeStruct((n, n, ctot), F32)],
        scratch_shapes=[pltpu.VMEM((n, n, V7X_LANES), F32)] + [pltpu.VMEM((n, V7X_LANES), F32)] * 3,
        compiler_params=_cparams(("parallel", "arbitrary"), 12 * tt * n * V7X_LANES * 4 + 5 * n * n * V7X_LANES * 4 + (8 << 20)),
        name="rw_scan",
    )(rT, wT, kT, vT, aT, *params, s0)


def _mix_kernel(osb_ref, orw_ref, g_ref, gsb_ref, grw_ref, x_ref, wsb_ref, wrw_ref, wout_ref, h_ref):
    sb = _dot(osb_ref[...].astype(BF16), wsb_ref[...])
    rw = _dot((orw_ref[...] * g_ref[...]).astype(BF16), wrw_ref[...])
    mixed = _sigmoid(gsb_ref[...]) * sb + _sigmoid(grw_ref[...]) * rw
    h_ref[...] = x_ref[...] + _dot(mixed.astype(BF16), wout_ref[...])


def _mix(o_sb, o_rw, g, gate_sb, gate_rw, x2d, wsb, wrw, wout, tm=256):
    n, d = x2d.shape
    c = o_sb.shape[1]
    assert n % tm == 0
    half = pl.BlockSpec((tm, c), lambda i: (i, 0))
    full = pl.BlockSpec((tm, d), lambda i: (i, 0))
    return pl.pallas_call(
        _mix_kernel,
        grid=(n // tm,),
        in_specs=[half, half, half, full, full, full, _resident(wsb.shape), _resident(wrw.shape), _resident(wout.shape)],
        out_specs=full,
        out_shape=jax.ShapeDtypeStruct((n, d), F32),
        compiler_params=_cparams(("parallel",), (wsb.size + wrw.size + wout.size) * 2 + 2 * tm * (3 * c + 4 * d) * 4 + (12 << 20)),
        name="branch_mix",
    )(o_sb, o_rw, g, gate_sb, gate_rw, x2d, wsb, wrw, wout)


PEER_RANKS = PEER_TOPK + 1
PEER_RANK_ROWS = 24


def _top_ranks(s, t_ref):
    t_ref[...] = jnp.full(t_ref.shape, NEG_BIG, F32)
    for rnk in range(PEER_RANKS):
        m = jnp.max(s, axis=0, keepdims=True)
        t_ref[rnk:rnk + 1, :] = m
        s = jnp.where(s == m, NEG_BIG, s)


def _peer_select_kernel(h_ref, g_ref, wq_ref, k1_ref, k2_ref, xn_o, ethr_o, coef_o, e2_o, t1_s, t2_s):
    xn = _rms(h_ref[...], g_ref[...]).astype(BF16)
    xn_o[...] = xn
    q = _dot(xn, wq_ref[...])
    k1h, k1l = _split_bf16(k1_ref[...])
    k2h, k2l = _split_bf16(k2_ref[...])
    sub = V7X_SUBLANES

    def scores(kh, kl, qh_):
        qh, ql = _split_bf16(qh_)
        return _dot_nt(kh, qh) + (_dot_nt(kh, ql) + _dot_nt(kl, qh))

    for h in range(PEER_HEADS):
        base = 2 * PEER_HALF * h
        s1 = scores(k1h, k1l, q[:, base:base + PEER_HALF])
        s2 = scores(k2h, k2l, q[:, base + PEER_HALF:base + 2 * PEER_HALF])
        _top_ranks(s1, t1_s)
        _top_ranks(s2, t2_s)
        t1 = t1_s[...]
        t2 = t2_s[...]
        cand = jnp.concatenate(
            [t1[0:1] + t2[0:sub], t1[0:1] + t2[sub:2 * sub], t1[0:1] + t2[2 * sub:3 * sub],
             t2[0:1] + t1[0:sub], t2[0:1] + t1[sub:2 * sub], t2[0:1] + t1[2 * sub:3 * sub],
             t1[1:2] + t2[0:sub], t2[1:2] + t1[0:sub],
             t1[2:3] + t2[0:sub], t1[3:4] + t2[0:sub], t1[4:5] + t2[0:sub]], axis=0)
        top = jnp.max(cand, axis=0, keepdims=True)
        zsum = jnp.zeros(top.shape, F32)
        lo16 = top
        for rnk in range(PEER_RANKS):
            m = jnp.max(cand, axis=0, keepdims=True)
            if rnk < PEER_TOPK:
                zsum = zsum + jnp.exp(m - top)
                lo16 = m
            else:
                tau = 0.5 * (lo16 + m)
            cand = jnp.where(cand == m, NEG_BIG, cand)
        ethr_o[h] = jnp.exp((tau - t2[0:1]) - s1)
        coef_o[h] = jnp.exp(s1 - t1[0:1]) / zsum
        e2_o[h] = jnp.exp(s2 - t2[0:1])


def _peer_select(h2d, g, wq, k1, k2, tn):
    n, d = h2d.shape
    assert n % tn == 0
    sel = pl.BlockSpec((PEER_HEADS, PEER_NKEYS, tn), lambda i: (0, 0, i))
    sel_shape = jax.ShapeDtypeStruct((PEER_HEADS, PEER_NKEYS, n), F32)
    return pl.pallas_call(
        _peer_select_kernel,
        grid=(n // tn,),
        in_specs=[pl.BlockSpec((tn, d), lambda i: (i, 0)), _resident((1, d)), _resident(wq.shape),
                  _resident(k1.shape), _resident(k2.shape)],
        out_specs=[pl.BlockSpec((tn, d), lambda i: (i, 0)), sel, sel, sel],
        out_shape=[jax.ShapeDtypeStruct((n, d), BF16), sel_shape, sel_shape, sel_shape],
        scratch_shapes=[pltpu.VMEM((PEER_RANK_ROWS, tn), F32)] * 2,
        compiler_params=_cparams(("parallel",), wq.size * 2 + 8 * PEER_HEADS * PEER_NKEYS * tn * 4 + 8 * tn * d * 4 + (12 << 20)),
        name="peer_select",
    )(h2d, g.reshape(1, d), wq, k1, k2)


def _peer_main_kernel(xn_ref, u_ref, v_ref, ethr_ref, coef_ref, e2_ref, h_ref, g_ref, y_ref, acc):
    j = pl.program_id(1)
    eb = u_ref.shape[0]
    nib = eb // PEER_NKEYS

    @pl.when(j == 0)
    def _():
        acc[...] = jnp.zeros(acc.shape, F32)

    hid = _dot_nt(u_ref[...], xn_ref[...])
    parts = []
    for ii in range(nib):
        row = pl.ds(j * nib + ii, 1)
        wgt = jnp.zeros((PEER_NKEYS, hid.shape[1]), F32)
        for h in range(PEER_HEADS):
            e2 = e2_ref[h]
            picked = jnp.where(e2 > ethr_ref[h, row, :], e2, 0.0)
            wgt = wgt + coef_ref[h, row, :] * picked
        x = hid[ii * PEER_NKEYS:(ii + 1) * PEER_NKEYS, :]
        gelu = 0.5 * x * (1.0 + lax.erf(x * (2.0 ** -0.5)))
        parts.append((gelu * wgt).astype(BF16))
    gated = jnp.concatenate(parts, axis=0)
    acc[...] += _dot_tn(gated, v_ref[...])

    @pl.when(j == pl.num_programs(1) - 1)
    def _():
        y_ref[...] = _rms(h_ref[...] + acc[...], g_ref[...])


def _peer_main(xn, u, v, ethr, coef, e2, h2d, g, tn, eb):
    n, d = h2d.shape
    e = u.shape[0]
    assert n % tn == 0 and e % eb == 0 and eb % PEER_NKEYS == 0
    sel = pl.BlockSpec((PEER_HEADS, PEER_NKEYS, tn), lambda i, j: (0, 0, i))
    tok = pl.BlockSpec((tn, d), lambda i, j: (i, 0))
    tab = pl.BlockSpec((eb, d), lambda i, j: (j, 0))
    vmem = 2 * tn * d * 2 + 4 * eb * d * 2 + 6 * PEER_HEADS * PEER_NKEYS * tn * 4 + 5 * tn * d * 4 + (8 << 20)
    return pl.pallas_call(
        _peer_main_kernel,
        grid=(n // tn, e // eb),
        in_specs=[tok, tab, tab, sel, sel, sel, tok, _resident((1, d))],
        out_specs=tok,
        out_shape=jax.ShapeDtypeStruct((n, d), F32),
        scratch_shapes=[pltpu.VMEM((tn, d), F32)],
        compiler_params=_cparams(("parallel", "arbitrary"), vmem),
        name="peer_main",
    )(xn, u, v, ethr, coef, e2, h2d, g.reshape(1, d))


def _to_chain_major(x, heads):
    b, t, c = x.shape
    n = c // heads
    return x.reshape(b, t, heads, n).transpose(1, 3, 0, 2).reshape(t, n, b * heads)


def _from_chain_major(x, b):
    t, n, ch = x.shape
    heads = ch // b
    return x.reshape(t, n, b, heads).transpose(2, 0, 3, 1).reshape(b, t, heads * n)


def _head_param(p, heads, b):
    n = p.size // heads
    return jnp.tile(p.reshape(heads, n).T, (1, b))


def _layer(x, past_k, past_v, wkv0, shift0, wts, sb_hp, rw_tt, peer_tn, peer_eb):
    b, t, d = x.shape
    n_tok = b * t
    x2d = x.reshape(n_tok, d)
    sbw = wts["wq"].shape[1]
    rww = wts["w0"].shape[-1]
    rw_heads = rww // RW_HEAD_DIM
    proj = wts["w_rw"].shape[1]

    q, k, v = _norm_matmul(x2d, wts["norm1_g"], [wts["wq"], wts["wk"], wts["wv"]])
    (p_rw,) = _norm_matmul(x2d, wts["norm1_g"], [wts["w_rw"]])
    gate_sb, gate_rw = _norm_matmul(x2d, wts["norm1_g"], [wts["w_gsb"], wts["w_grw"]])

    q3, k3, v3 = (a.reshape(b, t, sbw) for a in (q, k, v))
    heads_sb = sbw // SB_HEAD_DIM
    if past_k is None:
        o_sb = _sb_attention(q3, k3, v3, k3, v3, prompt=True, hp=sb_hp)
        wkv0 = jnp.zeros((b, rw_heads, RW_HEAD_DIM, RW_HEAD_DIM), F32)
        shift0 = jnp.zeros((b, 1, proj), F32)
    else:
        pk = past_k.reshape(b, past_k.shape[1], sbw)
        pv = past_v.reshape(b, past_v.shape[1], sbw)
        o_sb = _sb_attention(q3, k3, v3, pk, pv, prompt=False, hp=sb_hp)

    p3 = p_rw.reshape(b, t, proj)
    r, w, kr, vr, a, g = _rw_prep(p3, shift0, wts["rw_mu"], wts["w0"], wts["a0"], wts["wa2"], wts["g2"], tt=min(t, 256))
    seqs = [_to_chain_major(s, rw_heads) for s in (r, w, kr, vr, a)]
    params = [_head_param(wts[nm], rw_heads, b) for nm in ("k_k", "k_a", "r_k", "ln_g", "ln_b")]
    s0 = wkv0.astype(F32).transpose(3, 2, 0, 1).reshape(RW_HEAD_DIM, RW_HEAD_DIM, b * rw_heads)
    oT, sT = _rw_scan(*seqs, params, s0, tt=rw_tt)
    o_rw = _from_chain_major(oT, b).reshape(n_tok, rww)
    wkv_new = sT.reshape(RW_HEAD_DIM, RW_HEAD_DIM, b, rw_heads).transpose(2, 3, 1, 0)
    shift_new = p3[:, t - 1:t, :]

    h = _mix(o_sb.reshape(n_tok, sbw), o_rw, g.reshape(n_tok, rww), gate_sb, gate_rw, x2d,
             wts["w_bsb"], wts["w_brw"], wts["w_out"])
    xn2, ethr, coef, e2 = _peer_select(h, wts["norm2_g"], wts["peer_wq"], wts["peer_k1"], wts["peer_k2"], tn=256)
    y = _peer_main(xn2, wts["peer_u"], wts["peer_v"], ethr, coef, e2, h, wts["final_g"], tn=peer_tn, eb=peer_eb)
    kv_shape = (b, t, heads_sb, SB_HEAD_DIM)
    return y.reshape(b, t, d), k.reshape(kv_shape), v.reshape(kv_shape), wkv_new, shift_new


def kernel(x_prompt, x_sample, cache_sb_k, cache_sb_v, state_rwkv_wkv, state_rwkv_shift, norm1_g, w_in, rw_mu, rw_w0, rw_w2, rw_a0, rw_a2, rw_g2, rw_k_k, rw_k_a, rw_r_k, rw_ln_g, rw_ln_b, w_branch_sb, w_branch_rw, w_out, norm2_g, peer_wq, peer_k1, peer_k2, peer_u, peer_v, final_norm_g):
    assert norm1_g.shape[0] == 1, "single-layer stack"
    d = x_prompt.shape[-1]
    sbw = w_branch_sb.shape[1]
    rww = w_branch_rw.shape[1]
    proj = rw_mu.shape[1]
    w = w_in[0]
    o = 0
    cols = {}
    for nm, width in (("wq", sbw), ("wk", sbw), ("wv", sbw), ("w_rw", proj), ("w_gsb", d), ("w_grw", d)):
        cols[nm] = w[:, o:o + width].astype(BF16)
        o += width
    zeros = jnp.zeros((RW_DECAY_RANK, rww), F32)
    wa2 = jnp.concatenate([jnp.concatenate([rw_w2[0], zeros], axis=1),
                           jnp.concatenate([zeros, rw_a2[0]], axis=1)], axis=0).astype(BF16)
    wts = dict(cols, norm1_g=norm1_g[0], rw_mu=rw_mu[0], w0=rw_w0[0], a0=rw_a0[0], wa2=wa2, g2=rw_g2[0].astype(BF16),
               k_k=rw_k_k[0], k_a=rw_k_a[0], r_k=rw_r_k[0].reshape(-1), ln_g=rw_ln_g[0], ln_b=rw_ln_b[0],
               w_bsb=w_branch_sb[0].astype(BF16), w_brw=w_branch_rw[0].astype(BF16), w_out=w_out[0].astype(BF16),
               norm2_g=norm2_g[0], peer_wq=peer_wq[0].astype(BF16), peer_k1=peer_k1[0], peer_k2=peer_k2[0],
               peer_u=peer_u[0].astype(BF16), peer_v=peer_v[0].astype(BF16), final_g=final_norm_g)

    sb_heads = sbw // SB_HEAD_DIM
    yp, kp, vp, wp, sp = _layer(x_prompt, None, None, None, None, wts, sb_hp=2, rw_tt=32, peer_tn=512, peer_eb=512)
    ys, ks, vs, ws, ss = _layer(x_sample, cache_sb_k[0], cache_sb_v[0], state_rwkv_wkv[0], state_rwkv_shift[0], wts,
                                sb_hp=sb_heads, rw_tt=32, peer_tn=512, peer_eb=512)
    st = lambda t: t[None]
    return (yp, ys, st(kp), st(vp), st(wp.astype(x_prompt.dtype)), st(sp), st(ks), st(vs), st(ws.astype(x_sample.dtype)), st(ss))
```

```python
import functools

import jax
import jax.numpy as jnp
from jax import lax
from jax.experimental import pallas as pl
from jax.experimental.pallas import tpu as pltpu

F32 = jnp.float32
BF16 = jnp.bfloat16

RMS_EPS = 1e-6
GN_EPS = 64e-5
SB_HEAD_DIM = 128
RW_HEAD_DIM = 64
RW_DECAY_RANK = 64
RW_A_RANK = 64
RW_GATE_RANK = 128
PEER_HEADS = 8
PEER_NKEYS = 128
PEER_HALF = 128
PEER_TOPK = 16

V7X_LANES = 128
V7X_SUBLANES = 8
V7X_VMEM_LIMIT_BYTES = 60000 * 1024

NEG_BIG = -1e30


def _cparams(semantics, vmem_bytes):
    return pltpu.CompilerParams(dimension_semantics=semantics,
                                vmem_limit_bytes=int(min(V7X_VMEM_LIMIT_BYTES, vmem_bytes)))


def _sigmoid(x):
    return 1.0 / (1.0 + jnp.exp(-x))


def _softplus(x):
    return jnp.maximum(x, 0.0) + jnp.log1p(jnp.exp(-jnp.abs(x)))


def _rms(x, g):
    return x * lax.rsqrt(jnp.mean(x * x, axis=-1, keepdims=True) + RMS_EPS) * g


def _dot(a, b):
    return jnp.dot(a, b, preferred_element_type=F32)


def _dot_nt(a, b):
    return lax.dot_general(a, b, (((1,), (1,)), ((), ())), preferred_element_type=F32)


def _dot_tn(a, b):
    return lax.dot_general(a, b, (((0,), (0,)), ((), ())), preferred_element_type=F32)


def _resident(shape):
    nd = len(shape)
    return pl.BlockSpec(shape, lambda *_: (0,) * nd, pipeline_mode=pl.Buffered(1))


def _norm_matmul_kernel(n_w, x_ref, g_ref, *refs):
    xn = _rms(x_ref[...], g_ref[...]).astype(BF16)
    for w_ref, o_ref in zip(refs[:n_w], refs[n_w:]):
        o_ref[...] = _dot(xn, w_ref[...])


def _norm_matmul(x2d, g, ws, tm=256):
    n, d = x2d.shape
    assert n % tm == 0
    wbytes = sum(w.size * 2 for w in ws)
    obytes = sum(2 * tm * w.shape[1] * 4 for w in ws)
    return pl.pallas_call(
        functools.partial(_norm_matmul_kernel, len(ws)),
        grid=(n // tm,),
        in_specs=[pl.BlockSpec((tm, d), lambda i: (i, 0)), _resident((1, d))] + [_resident(w.shape) for w in ws],
        out_specs=[pl.BlockSpec((tm, w.shape[1]), lambda i: (i, 0)) for w in ws],
        out_shape=[jax.ShapeDtypeStruct((n, w.shape[1]), F32) for w in ws],
        compiler_params=_cparams(("parallel",), wbytes + 2 * obytes + 6 * tm * d * 4 + (8 << 20)),
        name="norm_matmul",
    )(x2d, g.reshape(1, d), *ws)


SB_SUPER = 512
SB_SUB = 256
LOG2E = 1.4426950408889634


def _sb_kernel(n_past_static, hp, q_ref, kd_ref, vd_ref, kp_ref, vp_ref, o_ref, acc_s, q_s):
    bq = q_ref.shape[0]
    dh = SB_HEAD_DIM
    q_s[...] = (q_ref[...] * (dh ** -0.5 * LOG2E)).astype(BF16)

    def neg_tri(n):
        r = lax.broadcasted_iota(jnp.int32, (n, n), 0)
        c = lax.broadcasted_iota(jnp.int32, (n, n), 1)
        return jnp.where(r > c, -1.0, 0.0).astype(BF16)

    def superblock(h, kb, vb, c, masked, first):
        cols = slice(h * dh, (h + 1) * dh)
        ks = kb.shape[0]
        sub = min(SB_SUB, ks)
        nsub = ks // sub
        z = _dot_nt(q_s[:, cols], kb.astype(BF16))
        sp = jnp.maximum(z, 0.0) + jnp.log2(1.0 + jnp.exp2(-jnp.abs(z)))
        if masked:
            allowed = lax.broadcasted_iota(jnp.int32, (bq, ks), 1) < lax.broadcasted_iota(jnp.int32, (bq, ks), 0)
            sp = jnp.where(allowed, sp, 0.0)
        sp16 = sp.astype(BF16)
        stacked = jnp.concatenate([sp16[:, k * sub:(k + 1) * sub] for k in range(nsub)], axis=0)
        after = _dot(stacked, neg_tri(sub))
        ws = [None] * nsub
        for k in reversed(range(nsub)):
            kc = slice(k * sub, (k + 1) * sub)
            w = jnp.exp2(z[:, kc] - sp[:, kc] + after[k * bq:(k + 1) * bq] + c)
            if masked:
                w = jnp.where(allowed[:, kc], w, 0.0)
            ws[k] = w.astype(BF16)
            c = c - jnp.sum(sp[:, kc], axis=1, keepdims=True)
        pv = _dot(jnp.concatenate(ws, axis=1), vb.astype(BF16))
        if first:
            acc_s[:, cols] = pv
        else:
            acc_s[:, cols] += pv
        return c

    cs = tuple(superblock(h, kd_ref[:, h * dh:(h + 1) * dh], vd_ref[:, h * dh:(h + 1) * dh],
                          jnp.zeros((bq, 1), F32), True, True) for h in range(hp))

    n_past = pl.program_id(2) if n_past_static is None else n_past_static

    def body(it, cs):
        rows = pl.ds(pl.multiple_of((n_past - 1 - it) * SB_SUPER, SB_SUPER), SB_SUPER)
        def past(ref, h):
            return ref[rows, h * dh:(h + 1) * dh] if len(ref.shape) == 2 else ref[rows, h, :]

        return tuple(superblock(h, past(kp_ref, h), past(vp_ref, h), cs[h], False, False) for h in range(hp))

    lax.fori_loop(0, n_past, body, cs)
    o_ref[...] = acc_s[...]


def _sb_attention(q, kd, vd, kp, vp, prompt, hp):
    b, tq, width = q.shape
    heads = width // SB_HEAD_DIM
    tp = kp.shape[1]
    assert tp % SB_SUPER == 0 and heads % hp == 0
    if prompt:
        bq, n_past = SB_SUPER, None
    else:
        bq, n_past = tq, tp // SB_SUPER
    assert tq % bq == 0
    gw = hp * SB_HEAD_DIM
    qspec = pl.BlockSpec((None, bq, gw), lambda bi, h, i: (bi, i, h))
    if kp.ndim == 4:
        assert hp == heads
        pspec = pl.BlockSpec((None, tp, heads, SB_HEAD_DIM), lambda bi, h, i: (bi, 0, 0, 0))
    else:
        pspec = pl.BlockSpec((None, tp, gw), lambda bi, h, i: (bi, 0, h))
    return pl.pallas_call(
        functools.partial(_sb_kernel, n_past, hp),
        grid=(b, heads // hp, tq // bq),
        in_specs=[qspec, qspec, qspec, pspec, pspec],
        out_specs=qspec,
        out_shape=jax.ShapeDtypeStruct((b, tq, width), F32),
        scratch_shapes=[pltpu.VMEM((bq, gw), F32), pltpu.VMEM((bq, gw), BF16)],
        compiler_params=_cparams(("parallel", "parallel", "arbitrary"),
                                 4 * tp * gw * 4 + 10 * bq * gw * 4 + 12 * bq * SB_SUPER * 4 * hp + (8 << 20)),
        name="sb_attention",
    )(q, kd, vd, kp, vp)


def _rw_prep_kernel(rw_width, p_ref, sh_ref, mu_ref, w0_ref, a0_ref, wa2_ref, g2_ref,
                    r_o, w_o, k_o, v_o, a_o, g_o, carry):
    @pl.when(pl.program_id(1) == 0)
    def _():
        carry[...] = sh_ref[...]

    p = p_ref[...]
    tt = p.shape[0]
    row = lax.broadcasted_iota(jnp.int32, p.shape, 0)
    prev = jnp.where(row == 0, carry[...], pltpu.roll(p, 1, axis=0))
    carry[...] = p[tt - 1:tt, :]
    ps = p + (prev - p) * mu_ref[...]
    c = rw_width
    r_o[...] = ps[:, 0:c]
    k_o[...] = ps[:, c:2 * c]
    v_o[...] = ps[:, 2 * c:3 * c]
    xwa = ps[:, 3 * c:3 * c + RW_DECAY_RANK + RW_A_RANK]
    lane = lax.broadcasted_iota(jnp.int32, xwa.shape, 1)
    lowrank_in = jnp.where(lane < RW_DECAY_RANK, jnp.tanh(xwa), xwa).astype(BF16)
    wa = _dot(lowrank_in, wa2_ref[...])
    w = -_softplus(-(w0_ref[...] + wa[:, 0:c])) - 0.5
    w_o[...] = jnp.exp(-jnp.exp(w))
    a_o[...] = _sigmoid(a0_ref[...] + wa[:, c:2 * c])
    xg = ps[:, 3 * c + RW_DECAY_RANK + RW_A_RANK:]
    g_o[...] = _dot(_sigmoid(xg).astype(BF16), g2_ref[...])


def _rw_prep(p, shift0, mu, w0, a0, wa2, g2, tt):
    b, t, proj = p.shape
    c = w0.shape[-1]
    assert t % tt == 0
    tok = pl.BlockSpec((None, tt, c), lambda bi, j: (bi, j, 0))
    return pl.pallas_call(
        functools.partial(_rw_prep_kernel, c),
        grid=(b, t // tt),
        in_specs=[pl.BlockSpec((None, tt, proj), lambda bi, j: (bi, j, 0)),
                  pl.BlockSpec((None, 1, proj), lambda bi, j: (bi, 0, 0)),
                  _resident((1, proj)), _resident((1, c)), _resident((1, c)),
                  _resident(wa2.shape), _resident(g2.shape)],
        out_specs=[tok] * 6,
        out_shape=[jax.ShapeDtypeStruct((b, t, c), F32)] * 6,
        scratch_shapes=[pltpu.VMEM((1, proj), F32)],
        compiler_params=_cparams(("parallel", "arbitrary"), 2 * tt * proj * 4 * 4 + 12 * tt * c * 4 + (8 << 20)),
        name="rw_prep",
    )(p, shift0, mu.reshape(1, proj), w0.reshape(1, c), a0.reshape(1, c), wa2, g2)


def _rw_scan_kernel(r_ref, w_ref, k_ref, v_ref, a_ref, kk_p, ka_p, rk_p, lng_p, lnb_p, s0_ref,
                    o_ref, S, kk_s, b_s, kt_s):
    j = pl.program_id(1)
    nk = S.shape[0]
    tt = r_ref.shape[0]

    @pl.when(j == 0)
    def _():
        S[...] = s0_ref[...]

    def step(t, carry):
        r = r_ref[t]
        k = k_ref[t]
        v = v_ref[t]
        a = a_ref[t]
        kk = k * kk_p[...]
        nrm = jnp.sqrt(jnp.sum(kk * kk, axis=0, keepdims=True))
        kk = kk / jnp.maximum(nrm, 1e-12)
        kt = k * (1.0 + (a - 1.0) * ka_p[...])
        kk_s[...] = kk
        b_s[...] = kk * a
        kt_s[...] = kt

        def p1(kx, sa):
            return sa + S[kx] * kk_s[pl.ds(kx, 1), :]

        sa = lax.fori_loop(0, nk, p1, jnp.zeros(v.shape, F32), unroll=8)

        def p2(kx, y):
            row = pl.ds(kx, 1)
            s_new = S[kx] * w_ref[t, row, :] - sa * b_s[row, :] + v * kt_s[row, :]
            S[kx] = s_new
            return y + s_new * r_ref[t, row, :]

        y = lax.fori_loop(0, nk, p2, jnp.zeros(v.shape, F32), unroll=8)
        mean = jnp.mean(y, axis=0, keepdims=True)
        d = y - mean
        var = jnp.mean(d * d, axis=0, keepdims=True)
        bonus = jnp.sum(r * kt * rk_p[...], axis=0, keepdims=True) * v
        o_ref[t] = d * lax.rsqrt(var + GN_EPS) * lng_p[...] + lnb_p[...] + bonus
        return carry

    lax.fori_loop(0, tt, step, 0)


def _rw_scan(rT, wT, kT, vT, aT, params, s0, tt):
    t, n, ctot = rT.shape
    assert t % tt == 0 and ctot % V7X_LANES == 0
    seq = pl.BlockSpec((tt, n, V7X_LANES), lambda c, j: (j, 0, c))
    par = pl.BlockSpec((n, V7X_LANES), lambda c, j: (0, c))
    st = pl.BlockSpec((n, n, V7X_LANES), lambda c, j: (0, 0, c))
    return pl.pallas_call(
        _rw_scan_kernel,
        grid=(ctot // V7X_LANES, t // tt),
        in_specs=[seq] * 5 + [par] * 5 + [st],
        out_specs=[seq, st],
        out_shape=[jax.ShapeDtypeStruct((t, n, ctot), F32), jax.ShapeDtypeStruct((n, n, ctot), F32)],
        scratch_shapes=[pltpu.VMEM((n, V7X_LANES), F32)] * 3,
        compiler_params=_cparams(("parallel", "arbitrary"), 12 * tt * n * V7X_LANES * 4 + 5 * n * n * V7X_LANES * 4 + (8 << 20)),
        name="rw_scan",
    )(rT, wT, kT, vT, aT, *params, s0)


def _mix_kernel(osb_ref, orw_ref, g_ref, gsb_ref, grw_ref, x_ref, wsb_ref, wrw_ref, wout_ref, h_ref):
    sb = _dot(osb_ref[...].astype(BF16), wsb_ref[...])
    rw = _dot((orw_ref[...] * g_ref[...]).astype(BF16), wrw_ref[...])
    mixed = _sigmoid(gsb_ref[...]) * sb + _sigmoid(grw_ref[...]) * rw
    h_ref[...] = x_ref[...] + _dot(mixed.astype(BF16), wout_ref[...])


def _mix(o_sb, o_rw, g, gate_sb, gate_rw, x2d, wsb, wrw, wout, tm=256):
    n, d = x2d.shape
    c = o_sb.shape[1]
    assert n % tm == 0
    half = pl.BlockSpec((tm, c), lambda i: (i, 0))
    full = pl.BlockSpec((tm, d), lambda i: (i, 0))
    return pl.pallas_call(
        _mix_kernel,
        grid=(n // tm,),
        in_specs=[half, half, half, full, full, full, _resident(wsb.shape), _resident(wrw.shape), _resident(wout.shape)],
        out_specs=full,
        out_shape=jax.ShapeDtypeStruct((n, d), F32),
        compiler_params=_cparams(("parallel",), (wsb.size + wrw.size + wout.size) * 2 + 2 * tm * (3 * c + 4 * d) * 4 + (12 << 20)),
        name="branch_mix",
    )(o_sb, o_rw, g, gate_sb, gate_rw, x2d, wsb, wrw, wout)


PEER_RANKS = PEER_TOPK + 1
PEER_RANK_ROWS = 24
PEER_UNRANKED = 127.0


def _top_ranks(s, t_ref, want_ranks):
    t_ref[...] = jnp.full(t_ref.shape, NEG_BIG, F32)
    ranks = jnp.full(s.shape, PEER_UNRANKED, F32) if want_ranks else None
    for rnk in range(PEER_RANKS):
        m = jnp.max(s, axis=0, keepdims=True)
        t_ref[rnk:rnk + 1, :] = m
        hit = s == m
        if want_ranks:
            ranks = jnp.where(hit, float(rnk), ranks)
        s = jnp.where(hit, NEG_BIG, s)
    return ranks


def _peer_select_kernel(h_ref, g_ref, wq_ref, k1_ref, k2_ref, xn_o, lim_o, coef_o, rank2_o, e2_o, t1_s, t2_s):
    xn = _rms(h_ref[...], g_ref[...]).astype(BF16)
    xn_o[...] = xn
    q = _dot(xn, wq_ref[...])
    k1 = k1_ref[...].astype(BF16)
    k2 = k2_ref[...].astype(BF16)
    sub = V7X_SUBLANES

    for h in range(PEER_HEADS):
        base = 2 * PEER_HALF * h
        s1 = _dot_nt(k1, q[:, base:base + PEER_HALF].astype(BF16))
        s2 = _dot_nt(k2, q[:, base + PEER_HALF:base + 2 * PEER_HALF].astype(BF16))
        _top_ranks(s1, t1_s, False)
        rank2 = _top_ranks(s2, t2_s, True)
        t1 = t1_s[...]
        t2 = t2_s[...]
        cand = jnp.concatenate(
            [t1[0:1] + t2[0:sub], t1[0:1] + t2[sub:2 * sub], t1[0:1] + t2[2 * sub:3 * sub],
             t2[0:1] + t1[0:sub], t2[0:1] + t1[sub:2 * sub], t2[0:1] + t1[2 * sub:3 * sub],
             t1[1:2] + t2[0:sub], t2[1:2] + t1[0:sub],
             t1[2:3] + t2[0:sub], t1[3:4] + t2[0:sub], t1[4:5] + t2[0:sub]], axis=0)
        top = jnp.max(cand, axis=0, keepdims=True)
        zsum = jnp.zeros(top.shape, F32)
        lo16 = top
        for rnk in range(PEER_RANKS):
            m = jnp.max(cand, axis=0, keepdims=True)
            if rnk < PEER_TOPK:
                zsum = zsum + jnp.exp(m - top)
                lo16 = m
            else:
                tau = 0.5 * (lo16 + m)
            cand = jnp.where(cand == m, NEG_BIG, cand)
        lim = jnp.zeros(s1.shape, F32)
        for a in range(PEER_TOPK):
            width = jnp.sum(jnp.where(t2 > tau - t1[a:a + 1], 1.0, 0.0), axis=0, keepdims=True)
            lim = jnp.where(s1 == t1[a:a + 1], width, lim)
        lim_o[h] = lim
        coef_o[h] = jnp.exp(s1 - t1[0:1]) / zsum
        rank2_o[h] = rank2.astype(BF16)
        e2_o[h] = jnp.exp(s2 - t2[0:1]).astype(BF16)


def _peer_select(h2d, g, wq, k1, k2, tn):
    n, d = h2d.shape
    assert n % tn == 0
    sel = pl.BlockSpec((PEER_HEADS, PEER_NKEYS, tn), lambda i: (0, 0, i))
    sel_f32 = jax.ShapeDtypeStruct((PEER_HEADS, PEER_NKEYS, n), F32)
    sel_bf16 = jax.ShapeDtypeStruct((PEER_HEADS, PEER_NKEYS, n), BF16)
    return pl.pallas_call(
        _peer_select_kernel,
        grid=(n // tn,),
        in_specs=[pl.BlockSpec((tn, d), lambda i: (i, 0)), _resident((1, d)), _resident(wq.shape),
                  _resident(k1.shape), _resident(k2.shape)],
        out_specs=[pl.BlockSpec((tn, d), lambda i: (i, 0)), sel, sel, sel, sel],
        out_shape=[jax.ShapeDtypeStruct((n, d), BF16), sel_f32, sel_f32, sel_bf16, sel_bf16],
        scratch_shapes=[pltpu.VMEM((PEER_RANK_ROWS, tn), F32)] * 2,
        compiler_params=_cparams(("parallel",), wq.size * 2 + 8 * PEER_HEADS * PEER_NKEYS * tn * 4 + 8 * tn * d * 4 + (12 << 20)),
        name="peer_select",
    )(h2d, g.reshape(1, d), wq, k1, k2)


def _peer_main_kernel(xn_ref, u_ref, v_ref, lim_ref, coef_ref, rank2_ref, e2_ref, h_ref, g_ref, y_ref, acc):
    j = pl.program_id(1)
    eb = u_ref.shape[0]
    nib = eb // PEER_NKEYS
    tile = (PEER_NKEYS, xn_ref.shape[0])

    @pl.when(j == 0)
    def _():
        acc[...] = jnp.zeros(acc.shape, F32)

    hid = _dot_nt(u_ref[...], xn_ref[...])
    parts = []
    for ii in range(nib):
        row = pl.ds(j * nib + ii, 1)
        wgt = jnp.zeros(tile, BF16)
        for h in range(PEER_HEADS):
            lim = jnp.broadcast_to(lim_ref[h, row, :], tile).astype(BF16)
            coef = jnp.broadcast_to(coef_ref[h, row, :], tile).astype(BF16)
            picked = jnp.where(rank2_ref[h] < lim, e2_ref[h], jnp.zeros((), BF16))
            wgt = wgt + coef * picked
        x = hid[ii * PEER_NKEYS:(ii + 1) * PEER_NKEYS, :].astype(BF16)
        gelu = (0.5 * x) * (1.0 + lax.erf(x * (2.0 ** -0.5)))
        parts.append(gelu * wgt)
    gated = jnp.concatenate(parts, axis=0)
    acc[...] += _dot_tn(gated, v_ref[...])

    @pl.when(j == pl.num_programs(1) - 1)
    def _():
        y_ref[...] = _rms(h_ref[...] + acc[...], g_ref[...])


def _peer_main(xn, u, v, lim, coef, rank2, e2, h2d, g, tn, eb):
    n, d = h2d.shape
    e = u.shape[0]
    assert n % tn == 0 and e % eb == 0 and eb % PEER_NKEYS == 0
    sel = pl.BlockSpec((PEER_HEADS, PEER_NKEYS, tn), lambda i, j: (0, 0, i))
    tok = pl.BlockSpec((tn, d), lambda i, j: (i, 0))
    tab = pl.BlockSpec((eb, d), lambda i, j: (j, 0))
    vmem = 2 * tn * d * 2 + 4 * eb * d * 2 + 6 * PEER_HEADS * PEER_NKEYS * tn * 4 + 5 * tn * d * 4 + (8 << 20)
    return pl.pallas_call(
        _peer_main_kernel,
        grid=(n // tn, e // eb),
        in_specs=[tok, tab, tab, sel, sel, sel, sel, tok, _resident((1, d))],
        out_specs=tok,
        out_shape=jax.ShapeDtypeStruct((n, d), F32),
        scratch_shapes=[pltpu.VMEM((tn, d), F32)],
        compiler_params=_cparams(("parallel", "arbitrary"), vmem),
        name="peer_main",
    )(xn, u, v, lim, coef, rank2, e2, h2d, g.reshape(1, d))


def _to_chain_major(x, heads):
    b, t, c = x.shape
    n = c // heads
    return x.reshape(b, t, heads, n).transpose(1, 3, 0, 2).reshape(t, n, b * heads)


def _from_chain_major(x, b):
    t, n, ch = x.shape
    heads = ch // b
    return x.reshape(t, n, b, heads).transpose(2, 0, 3, 1).reshape(b, t, heads * n)


def _head_param(p, heads, b):
    n = p.size // heads
    return jnp.tile(p.reshape(heads, n).T, (1, b))


def _layer(x, past_k, past_v, wkv0, shift0, wts, sb_hp, rw_tt, peer_tn, peer_eb):
    b, t, d = x.shape
    n_tok = b * t
    x2d = x.reshape(n_tok, d)
    sbw = wts["wq"].shape[1]
    rww = wts["w0"].shape[-1]
    rw_heads = rww // RW_HEAD_DIM
    proj = wts["w_rw"].shape[1]

    q, k, v = _norm_matmul(x2d, wts["norm1_g"], [wts["wq"], wts["wk"], wts["wv"]])
    (p_rw,) = _norm_matmul(x2d, wts["norm1_g"], [wts["w_rw"]])
    gate_sb, gate_rw = _norm_matmul(x2d, wts["norm1_g"], [wts["w_gsb"], wts["w_grw"]])

    q3, k3, v3 = (a.reshape(b, t, sbw) for a in (q, k, v))
    heads_sb = sbw // SB_HEAD_DIM
    if past_k is None:
        o_sb = _sb_attention(q3, k3, v3, k3, v3, prompt=True, hp=sb_hp)
        wkv0 = jnp.zeros((b, rw_heads, RW_HEAD_DIM, RW_HEAD_DIM), F32)
        shift0 = jnp.zeros((b, 1, proj), F32)
    else:
        o_sb = _sb_attention(q3, k3, v3, past_k, past_v, prompt=False, hp=sb_hp)

    p3 = p_rw.reshape(b, t, proj)
    r, w, kr, vr, a, g = _rw_prep(p3, shift0, wts["rw_mu"], wts["w0"], wts["a0"], wts["wa2"], wts["g2"], tt=min(t, 256))
    seqs = [_to_chain_major(s, rw_heads) for s in (r, w, kr, vr, a)]
    params = [_head_param(wts[nm], rw_heads, b) for nm in ("k_k", "k_a", "r_k", "ln_g", "ln_b")]
    s0 = wkv0.astype(F32).transpose(3, 2, 0, 1).reshape(RW_HEAD_DIM, RW_HEAD_DIM, b * rw_heads)
    oT, sT = _rw_scan(*seqs, params, s0, tt=rw_tt)
    o_rw = _from_chain_major(oT, b).reshape(n_tok, rww)
    wkv_new = sT.reshape(RW_HEAD_DIM, RW_HEAD_DIM, b, rw_heads).transpose(2, 3, 1, 0)
    shift_new = p3[:, t - 1:t, :]

    h = _mix(o_sb.reshape(n_tok, sbw), o_rw, g.reshape(n_tok, rww), gate_sb, gate_rw, x2d,
             wts["w_bsb"], wts["w_brw"], wts["w_out"])
    xn2, lim, coef, rank2, e2 = _peer_select(h, wts["norm2_g"], wts["peer_wq"], wts["peer_k1"], wts["peer_k2"], tn=256)
    y = _peer_main(xn2, wts["peer_u"], wts["peer_v"], lim, coef, rank2, e2, h, wts["final_g"], tn=peer_tn, eb=peer_eb)
    kv_shape = (b, t, heads_sb, SB_HEAD_DIM)
    return y.reshape(b, t, d), k.reshape(kv_shape), v.reshape(kv_shape), wkv_new, shift_new


def kernel(x_prompt, x_sample, cache_sb_k, cache_sb_v, state_rwkv_wkv, state_rwkv_shift, norm1_g, w_in, rw_mu, rw_w0, rw_w2, rw_a0, rw_a2, rw_g2, rw_k_k, rw_k_a, rw_r_k, rw_ln_g, rw_ln_b, w_branch_sb, w_branch_rw, w_out, norm2_g, peer_wq, peer_k1, peer_k2, peer_u, peer_v, final_norm_g):
    assert norm1_g.shape[0] == 1, "single-layer stack"
    d = x_prompt.shape[-1]
    sbw = w_branch_sb.shape[1]
    rww = w_branch_rw.shape[1]
    proj = rw_mu.shape[1]
    w = w_in[0]
    o = 0
    cols = {}
    for nm, width in (("wq", sbw), ("wk", sbw), ("wv", sbw), ("w_rw", proj), ("w_gsb", d), ("w_grw", d)):
        cols[nm] = w[:, o:o + width].astype(BF16)
        o += width
    zeros = jnp.zeros((RW_DECAY_RANK, rww), F32)
    wa2 = jnp.concatenate([jnp.concatenate([rw_w2[0], zeros], axis=1),
                           jnp.concatenate([zeros, rw_a2[0]], axis=1)], axis=0).astype(BF16)
    wts = dict(cols, norm1_g=norm1_g[0], rw_mu=rw_mu[0], w0=rw_w0[0], a0=rw_a0[0], wa2=wa2, g2=rw_g2[0].astype(BF16),
               k_k=rw_k_k[0], k_a=rw_k_a[0], r_k=rw_r_k[0].reshape(-1), ln_g=rw_ln_g[0], ln_b=rw_ln_b[0],
               w_bsb=w_branch_sb[0].astype(BF16), w_brw=w_branch_rw[0].astype(BF16), w_out=w_out[0].astype(BF16),
               norm2_g=norm2_g[0], peer_wq=peer_wq[0].astype(BF16), peer_k1=peer_k1[0], peer_k2=peer_k2[0],
               peer_u=peer_u[0].astype(BF16), peer_v=peer_v[0].astype(BF16), final_g=final_norm_g)

    sb_heads = sbw // SB_HEAD_DIM
    yp, kp, vp, wp, sp = _layer(x_prompt, None, None, None, None, wts, sb_hp=2, rw_tt=32, peer_tn=512, peer_eb=512)
    ys, ks, vs, ws, ss = _layer(x_sample, cache_sb_k[0], cache_sb_v[0], state_rwkv_wkv[0], state_rwkv_shift[0], wts,
                                sb_hp=sb_heads, rw_tt=32, peer_tn=512, peer_eb=512)
    st = lambda t: t[None]
    return (yp, ys, st(kp), st(vp), st(wp.astype(x_prompt.dtype)), st(sp), st(ks), st(vs), st(ws.astype(x_sample.dtype)), st(ss))
```

```python
import functools

import jax
import jax.numpy as jnp
from jax import lax
from jax.experimental import pallas as pl
from jax.experimental.pallas import tpu as pltpu

F32 = jnp.float32
BF16 = jnp.bfloat16

RMS_EPS = 1e-6
GN_EPS = 64e-5
SB_HEAD_DIM = 128
RW_HEAD_DIM = 64
RW_DECAY_RANK = 64
RW_A_RANK = 64
RW_GATE_RANK = 128
PEER_HEADS = 8
PEER_NKEYS = 128
PEER_HALF = 128
PEER_TOPK = 16

V7X_LANES = 128
V7X_SUBLANES = 8
V7X_VMEM_LIMIT_BYTES = 60000 * 1024

NEG_BIG = -1e30


def _cparams(semantics, vmem_bytes):
    return pltpu.CompilerParams(dimension_semantics=semantics,
                                vmem_limit_bytes=int(min(V7X_VMEM_LIMIT_BYTES, vmem_bytes)))


def _sigmoid(x):
    return 1.0 / (1.0 + jnp.exp(-x))


def _softplus(x):
    return jnp.maximum(x, 0.0) + jnp.log1p(jnp.exp(-jnp.abs(x)))


def _rms(x, g):
    return x * lax.rsqrt(jnp.mean(x * x, axis=-1, keepdims=True) + RMS_EPS) * g


def _dot(a, b):
    return jnp.dot(a, b, preferred_element_type=F32)


def _dot_nt(a, b):
    return lax.dot_general(a, b, (((1,), (1,)), ((), ())), preferred_element_type=F32)


def _dot_tn(a, b):
    return lax.dot_general(a, b, (((0,), (0,)), ((), ())), preferred_element_type=F32)


def _resident(shape):
    nd = len(shape)
    return pl.BlockSpec(shape, lambda *_: (0,) * nd, pipeline_mode=pl.Buffered(1))


def _norm_matmul_kernel(n_w, x_ref, g_ref, *refs):
    xn = _rms(x_ref[...], g_ref[...]).astype(BF16)
    for w_ref, o_ref in zip(refs[:n_w], refs[n_w:]):
        o_ref[...] = _dot(xn, w_ref[...])


def _norm_matmul(x2d, g, ws, tm=256):
    n, d = x2d.shape
    assert n % tm == 0
    wbytes = sum(w.size * 2 for w in ws)
    obytes = sum(2 * tm * w.shape[1] * 4 for w in ws)
    return pl.pallas_call(
        functools.partial(_norm_matmul_kernel, len(ws)),
        grid=(n // tm,),
        in_specs=[pl.BlockSpec((tm, d), lambda i: (i, 0)), _resident((1, d))] + [_resident(w.shape) for w in ws],
        out_specs=[pl.BlockSpec((tm, w.shape[1]), lambda i: (i, 0)) for w in ws],
        out_shape=[jax.ShapeDtypeStruct((n, w.shape[1]), F32) for w in ws],
        compiler_params=_cparams(("parallel",), wbytes + 2 * obytes + 6 * tm * d * 4 + (8 << 20)),
        name="norm_matmul",
    )(x2d, g.reshape(1, d), *ws)


SB_SUPER = 512
SB_SUB = 256
LOG2E = 1.4426950408889634


def _sb_kernel(n_past_static, hp, q_ref, kd_ref, vd_ref, kp_ref, vp_ref, o_ref, acc_s, q_s):
    bq = q_ref.shape[0]
    dh = SB_HEAD_DIM
    q_s[...] = (q_ref[...] * (dh ** -0.5 * LOG2E)).astype(BF16)

    def neg_tri(n):
        r = lax.broadcasted_iota(jnp.int32, (n, n), 0)
        c = lax.broadcasted_iota(jnp.int32, (n, n), 1)
        return jnp.where(r > c, -1.0, 0.0).astype(BF16)

    def superblock(h, kb, vb, c, masked, first):
        cols = slice(h * dh, (h + 1) * dh)
        ks = kb.shape[0]
        sub = min(SB_SUB, ks)
        nsub = ks // sub
        z = _dot_nt(q_s[:, cols], kb.astype(BF16))
        sp = jnp.maximum(z, 0.0) + jnp.log2(1.0 + jnp.exp2(-jnp.abs(z)))
        if masked:
            allowed = lax.broadcasted_iota(jnp.int32, (bq, ks), 1) < lax.broadcasted_iota(jnp.int32, (bq, ks), 0)
            sp = jnp.where(allowed, sp, 0.0)
        sp16 = sp.astype(BF16)
        stacked = jnp.concatenate([sp16[:, k * sub:(k + 1) * sub] for k in range(nsub)], axis=0)
        after = _dot(stacked, neg_tri(sub))
        ws = [None] * nsub
        for k in reversed(range(nsub)):
            kc = slice(k * sub, (k + 1) * sub)
            w = jnp.exp2(z[:, kc] - sp[:, kc] + after[k * bq:(k + 1) * bq] + c)
            if masked:
                w = jnp.where(allowed[:, kc], w, 0.0)
            ws[k] = w.astype(BF16)
            c = c - jnp.sum(sp[:, kc], axis=1, keepdims=True)
        pv = _dot(jnp.concatenate(ws, axis=1), vb.astype(BF16))
        if first:
            acc_s[:, cols] = pv
        else:
            acc_s[:, cols] += pv
        return c

    cs = tuple(superblock(h, kd_ref[:, h * dh:(h + 1) * dh], vd_ref[:, h * dh:(h + 1) * dh],
                          jnp.zeros((bq, 1), F32), True, True) for h in range(hp))

    n_past = pl.program_id(2) if n_past_static is None else n_past_static

    def body(it, cs):
        rows = pl.ds(pl.multiple_of((n_past - 1 - it) * SB_SUPER, SB_SUPER), SB_SUPER)
        if len(kp_ref.shape) == 2:
            kbs = [kp_ref[rows, h * dh:(h + 1) * dh] for h in range(hp)]
            vbs = [vp_ref[rows, h * dh:(h + 1) * dh] for h in range(hp)]
        else:
            kbs = pltpu.einshape("thd->htd", kp_ref[rows])
            vbs = pltpu.einshape("thd->htd", vp_ref[rows])
        return tuple(superblock(h, kbs[h], vbs[h], cs[h], False, False) for h in range(hp))

    lax.fori_loop(0, n_past, body, cs)
    o_ref[...] = acc_s[...]


def _sb_attention(q, kd, vd, kp, vp, prompt, hp):
    b, tq, width = q.shape
    heads = width // SB_HEAD_DIM
    tp = kp.shape[1]
    assert tp % SB_SUPER == 0 and heads % hp == 0
    if prompt:
        bq, n_past = SB_SUPER, None
    else:
        bq, n_past = tq, tp // SB_SUPER
    assert tq % bq == 0
    gw = hp * SB_HEAD_DIM
    qspec = pl.BlockSpec((None, bq, gw), lambda bi, h, i: (bi, i, h))
    if kp.ndim == 4:
        assert hp == heads
        pspec = pl.BlockSpec((None, tp, heads, SB_HEAD_DIM), lambda bi, h, i: (bi, 0, 0, 0))
    else:
        pspec = pl.BlockSpec((None, tp, gw), lambda bi, h, i: (bi, 0, h))
    return pl.pallas_call(
        functools.partial(_sb_kernel, n_past, hp),
        grid=(b, heads // hp, tq // bq),
        in_specs=[qspec, qspec, qspec, pspec, pspec],
        out_specs=qspec,
        out_shape=jax.ShapeDtypeStruct((b, tq, width), F32),
        scratch_shapes=[pltpu.VMEM((bq, gw), F32), pltpu.VMEM((bq, gw), BF16)],
        compiler_params=_cparams(("parallel", "parallel", "arbitrary"),
                                 4 * tp * gw * 4 + 10 * bq * gw * 4 + 12 * bq * SB_SUPER * 4 * hp + (8 << 20)),
        name="sb_attention",
    )(q, kd, vd, kp, vp)


def _rw_prep_kernel(rw_width, p_ref, sh_ref, mu_ref, w0_ref, a0_ref, wa2_ref, g2_ref,
                    r_o, w_o, k_o, v_o, a_o, g_o, carry):
    @pl.when(pl.program_id(1) == 0)
    def _():
        carry[...] = sh_ref[...]

    p = p_ref[...]
    tt = p.shape[0]
    row = lax.broadcasted_iota(jnp.int32, p.shape, 0)
    prev = jnp.where(row == 0, carry[...], pltpu.roll(p, 1, axis=0))
    carry[...] = p[tt - 1:tt, :]
    ps = p + (prev - p) * mu_ref[...]
    c = rw_width
    r_o[...] = ps[:, 0:c]
    k_o[...] = ps[:, c:2 * c]
    v_o[...] = ps[:, 2 * c:3 * c]
    xwa = ps[:, 3 * c:3 * c + RW_DECAY_RANK + RW_A_RANK]
    lane = lax.broadcasted_iota(jnp.int32, xwa.shape, 1)
    lowrank_in = jnp.where(lane < RW_DECAY_RANK, jnp.tanh(xwa), xwa).astype(BF16)
    wa = _dot(lowrank_in, wa2_ref[...])
    w = -_softplus(-(w0_ref[...] + wa[:, 0:c])) - 0.5
    w_o[...] = jnp.exp(-jnp.exp(w))
    a_o[...] = _sigmoid(a0_ref[...] + wa[:, c:2 * c])
    xg = ps[:, 3 * c + RW_DECAY_RANK + RW_A_RANK:]
    g_o[...] = _dot(_sigmoid(xg).astype(BF16), g2_ref[...])


def _rw_prep(p, shift0, mu, w0, a0, wa2, g2, tt):
    b, t, proj = p.shape
    c = w0.shape[-1]
    assert t % tt == 0
    tok = pl.BlockSpec((None, tt, c), lambda bi, j: (bi, j, 0))
    return pl.pallas_call(
        functools.partial(_rw_prep_kernel, c),
        grid=(b, t // tt),
        in_specs=[pl.BlockSpec((None, tt, proj), lambda bi, j: (bi, j, 0)),
                  pl.BlockSpec((None, 1, proj), lambda bi, j: (bi, 0, 0)),
                  _resident((1, proj)), _resident((1, c)), _resident((1, c)),
                  _resident(wa2.shape), _resident(g2.shape)],
        out_specs=[tok] * 6,
        out_shape=[jax.ShapeDtypeStruct((b, t, c), F32)] * 6,
        scratch_shapes=[pltpu.VMEM((1, proj), F32)],
        compiler_params=_cparams(("parallel", "arbitrary"), 2 * tt * proj * 4 * 4 + 12 * tt * c * 4 + (8 << 20)),
        name="rw_prep",
    )(p, shift0, mu.reshape(1, proj), w0.reshape(1, c), a0.reshape(1, c), wa2, g2)


def _rw_scan_kernel(r_ref, w_ref, k_ref, v_ref, a_ref, kk_p, ka_p, rk_p, lng_p, lnb_p, s0_ref,
                    o_ref, S, kk_s, b_s, kt_s):
    j = pl.program_id(1)
    nk = S.shape[0]
    tt = r_ref.shape[0]

    @pl.when(j == 0)
    def _():
        S[...] = s0_ref[...]

    def step(t, carry):
        r = r_ref[t]
        k = k_ref[t]
        v = v_ref[t]
        a = a_ref[t]
        kk = k * kk_p[...]
        nrm = jnp.sqrt(jnp.sum(kk * kk, axis=0, keepdims=True))
        kk = kk / jnp.maximum(nrm, 1e-12)
        kt = k * (1.0 + (a - 1.0) * ka_p[...])
        kk_s[...] = kk
        b_s[...] = kk * a
        kt_s[...] = kt

        def p1(kx, sa):
            return sa + S[kx] * kk_s[pl.ds(kx, 1), :]

        sa = lax.fori_loop(0, nk, p1, jnp.zeros(v.shape, F32), unroll=8)

        def p2(kx, y):
            row = pl.ds(kx, 1)
            s_new = S[kx] * w_ref[t, row, :] - sa * b_s[row, :] + v * kt_s[row, :]
            S[kx] = s_new
            return y + s_new * r_ref[t, row, :]

        y = lax.fori_loop(0, nk, p2, jnp.zeros(v.shape, F32), unroll=8)
        mean = jnp.mean(y, axis=0, keepdims=True)
        d = y - mean
        var = jnp.mean(d * d, axis=0, keepdims=True)
        bonus = jnp.sum(r * kt * rk_p[...], axis=0, keepdims=True) * v
        o_ref[t] = d * lax.rsqrt(var + GN_EPS) * lng_p[...] + lnb_p[...] + bonus
        return carry

    lax.fori_loop(0, tt, step, 0)


def _rw_scan(rT, wT, kT, vT, aT, params, s0, tt):
    t, n, ctot = rT.shape
    assert t % tt == 0 and ctot % V7X_LANES == 0
    seq = pl.BlockSpec((tt, n, V7X_LANES), lambda c, j: (j, 0, c))
    par = pl.BlockSpec((n, V7X_LANES), lambda c, j: (0, c))
    st = pl.BlockSpec((n, n, V7X_LANES), lambda c, j: (0, 0, c))
    return pl.pallas_call(
        _rw_scan_kernel,
        grid=(ctot // V7X_LANES, t // tt),
        in_specs=[seq] * 5 + [par] * 5 + [st],
        out_specs=[seq, st],
        out_shape=[jax.ShapeDtypeStruct((t, n, ctot), F32), jax.ShapeDtypeStruct((n, n, ctot), F32)],
        scratch_shapes=[pltpu.VMEM((n, V7X_LANES), F32)] * 3,
        compiler_params=_cparams(("parallel", "arbitrary"), 12 * tt * n * V7X_LANES * 4 + 5 * n * n * V7X_LANES * 4 + (8 << 20)),
        name="rw_scan",
    )(rT, wT, kT, vT, aT, *params, s0)


def _mix_kernel(osb_ref, orw_ref, g_ref, gsb_ref, grw_ref, x_ref, wsb_ref, wrw_ref, wout_ref, h_ref):
    sb = _dot(osb_ref[...].astype(BF16), wsb_ref[...])
    rw = _dot((orw_ref[...] * g_ref[...]).astype(BF16), wrw_ref[...])
    mixed = _sigmoid(gsb_ref[...]) * sb + _sigmoid(grw_ref[...]) * rw
    h_ref[...] = x_ref[...] + _dot(mixed.astype(BF16), wout_ref[...])


def _mix(o_sb, o_rw, g, gate_sb, gate_rw, x2d, wsb, wrw, wout, tm=256):
    n, d = x2d.shape
    c = o_sb.shape[1]
    assert n % tm == 0
    half = pl.BlockSpec((tm, c), lambda i: (i, 0))
    full = pl.BlockSpec((tm, d), lambda i: (i, 0))
    return pl.pallas_call(
        _mix_kernel,
        grid=(n // tm,),
        in_specs=[half, half, half, full, full, full, _resident(wsb.shape), _resident(wrw.shape), _resident(wout.shape)],
        out_specs=full,
        out_shape=jax.ShapeDtypeStruct((n, d), F32),
        compiler_params=_cparams(("parallel",), (wsb.size + wrw.size + wout.size) * 2 + 2 * tm * (3 * c + 4 * d) * 4 + (12 << 20)),
        name="branch_mix",
    )(o_sb, o_rw, g, gate_sb, gate_rw, x2d, wsb, wrw, wout)


PEER_RANKS = PEER_TOPK + 1
PEER_RANK_ROWS = 24
PEER_UNRANKED = 127.0


def _top_ranks(s, t_ref, want_ranks):
    t_ref[...] = jnp.full(t_ref.shape, NEG_BIG, F32)
    ranks = jnp.full(s.shape, PEER_UNRANKED, F32) if want_ranks else None
    for rnk in range(PEER_RANKS):
        m = jnp.max(s, axis=0, keepdims=True)
        t_ref[rnk:rnk + 1, :] = m
        hit = s == m
        if want_ranks:
            ranks = jnp.where(hit, float(rnk), ranks)
        s = jnp.where(hit, NEG_BIG, s)
    return ranks


def _peer_select_kernel(h_ref, g_ref, wq_ref, k1_ref, k2_ref, xn_o, lim_o, coef_o, rank2_o, e2_o, t1_s, t2_s):
    xn = _rms(h_ref[...], g_ref[...]).astype(BF16)
    xn_o[...] = xn
    q = _dot(xn, wq_ref[...])
    k1 = k1_ref[...].astype(BF16)
    k2 = k2_ref[...].astype(BF16)
    sub = V7X_SUBLANES

    for h in range(PEER_HEADS):
        base = 2 * PEER_HALF * h
        s1 = _dot_nt(k1, q[:, base:base + PEER_HALF].astype(BF16))
        s2 = _dot_nt(k2, q[:, base + PEER_HALF:base + 2 * PEER_HALF].astype(BF16))
        _top_ranks(s1, t1_s, False)
        rank2 = _top_ranks(s2, t2_s, True)
        t1 = t1_s[...]
        t2 = t2_s[...]
        cand = jnp.concatenate(
            [t1[0:1] + t2[0:sub], t1[0:1] + t2[sub:2 * sub], t1[0:1] + t2[2 * sub:3 * sub],
             t2[0:1] + t1[0:sub], t2[0:1] + t1[sub:2 * sub], t2[0:1] + t1[2 * sub:3 * sub],
             t1[1:2] + t2[0:sub], t2[1:2] + t1[0:sub],
             t1[2:3] + t2[0:sub], t1[3:4] + t2[0:sub], t1[4:5] + t2[0:sub]], axis=0)
        top = jnp.max(cand, axis=0, keepdims=True)
        zsum = jnp.zeros(top.shape, F32)
        lo16 = top
        for rnk in range(PEER_RANKS):
            m = jnp.max(cand, axis=0, keepdims=True)
            if rnk < PEER_TOPK:
                zsum = zsum + jnp.exp(m - top)
                lo16 = m
            else:
                tau = 0.5 * (lo16 + m)
            cand = jnp.where(cand == m, NEG_BIG, cand)
        lim = jnp.zeros(s1.shape, F32)
        for a in range(PEER_TOPK):
            width = jnp.sum(jnp.where(t2 > tau - t1[a:a + 1], 1.0, 0.0), axis=0, keepdims=True)
            lim = jnp.where(s1 == t1[a:a + 1], width, lim)
        lim_o[h] = lim
        coef_o[h] = jnp.exp(s1 - t1[0:1]) / zsum
        rank2_o[h] = rank2.astype(BF16)
        e2_o[h] = jnp.exp(s2 - t2[0:1]).astype(BF16)


def _peer_select(h2d, g, wq, k1, k2, tn):
    n, d = h2d.shape
    assert n % tn == 0
    sel = pl.BlockSpec((PEER_HEADS, PEER_NKEYS, tn), lambda i: (0, 0, i))
    sel_f32 = jax.ShapeDtypeStruct((PEER_HEADS, PEER_NKEYS, n), F32)
    sel_bf16 = jax.ShapeDtypeStruct((PEER_HEADS, PEER_NKEYS, n), BF16)
    return pl.pallas_call(
        _peer_select_kernel,
        grid=(n // tn,),
        in_specs=[pl.BlockSpec((tn, d), lambda i: (i, 0)), _resident((1, d)), _resident(wq.shape),
                  _resident(k1.shape), _resident(k2.shape)],
        out_specs=[pl.BlockSpec((tn, d), lambda i: (i, 0)), sel, sel, sel, sel],
        out_shape=[jax.ShapeDtypeStruct((n, d), BF16), sel_f32, sel_f32, sel_bf16, sel_bf16],
        scratch_shapes=[pltpu.VMEM((PEER_RANK_ROWS, tn), F32)] * 2,
        compiler_params=_cparams(("parallel",), wq.size * 2 + 8 * PEER_HEADS * PEER_NKEYS * tn * 4 + 8 * tn * d * 4 + (12 << 20)),
        name="peer_select",
    )(h2d, g.reshape(1, d), wq, k1, k2)


def _peer_main_kernel(xn_ref, u_ref, v_ref, lim_ref, coef_ref, rank2_ref, e2_ref, h_ref, g_ref, y_ref, acc):
    j = pl.program_id(1)
    eb = u_ref.shape[0]
    nib = eb // PEER_NKEYS
    tile = (PEER_NKEYS, xn_ref.shape[0])

    @pl.when(j == 0)
    def _():
        acc[...] = jnp.zeros(acc.shape, F32)

    hid = _dot_nt(u_ref[...], xn_ref[...])
    parts = []
    for ii in range(nib):
        row = pl.ds(j * nib + ii, 1)
        wgt = jnp.zeros(tile, BF16)
        for h in range(PEER_HEADS):
            lim = jnp.broadcast_to(lim_ref[h, row, :], tile).astype(BF16)
            coef = jnp.broadcast_to(coef_ref[h, row, :], tile).astype(BF16)
            picked = jnp.where(rank2_ref[h] < lim, e2_ref[h], jnp.zeros((), BF16))
            wgt = wgt + coef * picked
        x = hid[ii * PEER_NKEYS:(ii + 1) * PEER_NKEYS, :].astype(BF16)
        gelu = (0.5 * x) * (1.0 + lax.erf(x * (2.0 ** -0.5)))
        parts.append(gelu * wgt)
    gated = jnp.concatenate(parts, axis=0)
    acc[...] += _dot_tn(gated, v_ref[...])

    @pl.when(j == pl.num_programs(1) - 1)
    def _():
        y_ref[...] = _rms(h_ref[...] + acc[...], g_ref[...])


def _peer_main(xn, u, v, lim, coef, rank2, e2, h2d, g, tn, eb):
    n, d = h2d.shape
    e = u.shape[0]
    assert n % tn == 0 and e % eb == 0 and eb % PEER_NKEYS == 0
    sel = pl.BlockSpec((PEER_HEADS, PEER_NKEYS, tn), lambda i, j: (0, 0, i))
    tok = pl.BlockSpec((tn, d), lambda i, j: (i, 0))
    tab = pl.BlockSpec((eb, d), lambda i, j: (j, 0))
    vmem = 2 * tn * d * 2 + 4 * eb * d * 2 + 6 * PEER_HEADS * PEER_NKEYS * tn * 4 + 5 * tn * d * 4 + (8 << 20)
    return pl.pallas_call(
        _peer_main_kernel,
        grid=(n // tn, e // eb),
        in_specs=[tok, tab, tab, sel, sel, sel, sel, tok, _resident((1, d))],
        out_specs=tok,
        out_shape=jax.ShapeDtypeStruct((n, d), F32),
        scratch_shapes=[pltpu.VMEM((tn, d), F32)],
        compiler_params=_cparams(("parallel", "arbitrary"), vmem),
        name="peer_main",
    )(xn, u, v, lim, coef, rank2, e2, h2d, g.reshape(1, d))


def _to_chain_major(x, heads):
    b, t, c = x.shape
    n = c // heads
    return x.reshape(b, t, heads, n).transpose(1, 3, 0, 2).reshape(t, n, b * heads)


def _from_chain_major(x, b):
    t, n, ch = x.shape
    heads = ch // b
    return x.reshape(t, n, b, heads).transpose(2, 0, 3, 1).reshape(b, t, heads * n)


def _head_param(p, heads, b):
    n = p.size // heads
    return jnp.tile(p.reshape(heads, n).T, (1, b))


def _layer(x, past_k, past_v, wkv0, shift0, wts, sb_hp, rw_tt, peer_tn, peer_eb):
    b, t, d = x.shape
    n_tok = b * t
    x2d = x.reshape(n_tok, d)
    sbw = wts["wq"].shape[1]
    rww = wts["w0"].shape[-1]
    rw_heads = rww // RW_HEAD_DIM
    proj = wts["w_rw"].shape[1]

    q, k, v = _norm_matmul(x2d, wts["norm1_g"], [wts["wq"], wts["wk"], wts["wv"]])
    (p_rw,) = _norm_matmul(x2d, wts["norm1_g"], [wts["w_rw"]])
    gate_sb, gate_rw = _norm_matmul(x2d, wts["norm1_g"], [wts["w_gsb"], wts["w_grw"]])

    q3, k3, v3 = (a.reshape(b, t, sbw) for a in (q, k, v))
    heads_sb = sbw // SB_HEAD_DIM
    if past_k is None:
        o_sb = _sb_attention(q3, k3, v3, k3, v3, prompt=True, hp=sb_hp)
        wkv0 = jnp.zeros((b, rw_heads, RW_HEAD_DIM, RW_HEAD_DIM), F32)
        shift0 = jnp.zeros((b, 1, proj), F32)
    else:
        o_sb = _sb_attention(q3, k3, v3, past_k, past_v, prompt=False, hp=sb_hp)

    p3 = p_rw.reshape(b, t, proj)
    r, w, kr, vr, a, g = _rw_prep(p3, shift0, wts["rw_mu"], wts["w0"], wts["a0"], wts["wa2"], wts["g2"], tt=min(t, 256))
    seqs = [_to_chain_major(s, rw_heads) for s in (r, w, kr, vr, a)]
    params = [_head_param(wts[nm], rw_heads, b) for nm in ("k_k", "k_a", "r_k", "ln_g", "ln_b")]
    s0 = wkv0.astype(F32).transpose(3, 2, 0, 1).reshape(RW_HEAD_DIM, RW_HEAD_DIM, b * rw_heads)
    oT, sT = _rw_scan(*seqs, params, s0, tt=rw_tt)
    o_rw = _from_chain_major(oT, b).reshape(n_tok, rww)
    wkv_new = sT.reshape(RW_HEAD_DIM, RW_HEAD_DIM, b, rw_heads).transpose(2, 3, 1, 0)
    shift_new = p3[:, t - 1:t, :]

    h = _mix(o_sb.reshape(n_tok, sbw), o_rw, g.reshape(n_tok, rww), gate_sb, gate_rw, x2d,
             wts["w_bsb"], wts["w_brw"], wts["w_out"])
    xn2, lim, coef, rank2, e2 = _peer_select(h, wts["norm2_g"], wts["peer_wq"], wts["peer_k1"], wts["peer_k2"], tn=256)
    y = _peer_main(xn2, wts["peer_u"], wts["peer_v"], lim, coef, rank2, e2, h, wts["final_g"], tn=peer_tn, eb=peer_eb)
    kv_shape = (b, t, heads_sb, SB_HEAD_DIM)
    return y.reshape(b, t, d), k.reshape(kv_shape), v.reshape(kv_shape), wkv_new, shift_new


def kernel(x_prompt, x_sample, cache_sb_k, cache_sb_v, state_rwkv_wkv, state_rwkv_shift, norm1_g, w_in, rw_mu, rw_w0, rw_w2, rw_a0, rw_a2, rw_g2, rw_k_k, rw_k_a, rw_r_k, rw_ln_g, rw_ln_b, w_branch_sb, w_branch_rw, w_out, norm2_g, peer_wq, peer_k1, peer_k2, peer_u, peer_v, final_norm_g):
    assert norm1_g.shape[0] == 1, "single-layer stack"
    d = x_prompt.shape[-1]
    sbw = w_branch_sb.shape[1]
    rww = w_branch_rw.shape[1]
    proj = rw_mu.shape[1]
    w = w_in[0]
    o = 0
    cols = {}
    for nm, width in (("wq", sbw), ("wk", sbw), ("wv", sbw), ("w_rw", proj), ("w_gsb", d), ("w_grw", d)):
        cols[nm] = w[:, o:o + width].astype(BF16)
        o += width
    zeros = jnp.zeros((RW_DECAY_RANK, rww), F32)
    wa2 = jnp.concatenate([jnp.concatenate([rw_w2[0], zeros], axis=1),
                           jnp.concatenate([zeros, rw_a2[0]], axis=1)], axis=0).astype(BF16)
    wts = dict(cols, norm1_g=norm1_g[0], rw_mu=rw_mu[0], w0=rw_w0[0], a0=rw_a0[0], wa2=wa2, g2=rw_g2[0].astype(BF16),
               k_k=rw_k_k[0], k_a=rw_k_a[0], r_k=rw_r_k[0].reshape(-1), ln_g=rw_ln_g[0], ln_b=rw_ln_b[0],
               w_bsb=w_branch_sb[0].astype(BF16), w_brw=w_branch_rw[0].astype(BF16), w_out=w_out[0].astype(BF16),
               norm2_g=norm2_g[0], peer_wq=peer_wq[0].astype(BF16), peer_k1=peer_k1[0], peer_k2=peer_k2[0],
               peer_u=peer_u[0].astype(BF16), peer_v=peer_v[0].astype(BF16), final_g=final_norm_g)

    sb_heads = sbw // SB_HEAD_DIM
    yp, kp, vp, wp, sp = _layer(x_prompt, None, None, None, None, wts, sb_hp=2, rw_tt=32, peer_tn=512, peer_eb=1024)
    ys, ks, vs, ws, ss = _layer(x_sample, cache_sb_k[0], cache_sb_v[0], state_rwkv_wkv[0], state_rwkv_shift[0], wts,
                                sb_hp=sb_heads, rw_tt=32, peer_tn=512, peer_eb=1024)
    st = lambda t: t[None]
    return (yp, ys, st(kp), st(vp), st(wp.astype(x_prompt.dtype)), st(sp), st(ks), st(vs), st(ws.astype(x_sample.dtype)), st(ss))
```

```python
import functools

import jax
import jax.numpy as jnp
from jax import lax
from jax.experimental import pallas as pl
from jax.experimental.pallas import tpu as pltpu

F32 = jnp.float32
BF16 = jnp.bfloat16

RMS_EPS = 1e-6
GN_EPS = 64e-5
SB_HEAD_DIM = 128
RW_HEAD_DIM = 64
RW_DECAY_RANK = 64
RW_A_RANK = 64
RW_GATE_RANK = 128
PEER_HEADS = 8
PEER_NKEYS = 128
PEER_HALF = 128
PEER_TOPK = 16

V7X_LANES = 128
V7X_SUBLANES = 8
V7X_VMEM_LIMIT_BYTES = 60000 * 1024

NEG_BIG = -1e30


def _cparams(semantics, vmem_bytes):
    return pltpu.CompilerParams(dimension_semantics=semantics,
                                vmem_limit_bytes=int(min(V7X_VMEM_LIMIT_BYTES, vmem_bytes)))


def _sigmoid(x):
    return 1.0 / (1.0 + jnp.exp(-x))


def _softplus(x):
    return jnp.maximum(x, 0.0) + jnp.log1p(jnp.exp(-jnp.abs(x)))


def _rms(x, g):
    return x * lax.rsqrt(jnp.mean(x * x, axis=-1, keepdims=True) + RMS_EPS) * g


def _dot(a, b):
    return jnp.dot(a, b, preferred_element_type=F32)


def _dot_nt(a, b):
    return lax.dot_general(a, b, (((1,), (1,)), ((), ())), preferred_element_type=F32)


def _dot_tn(a, b):
    return lax.dot_general(a, b, (((0,), (0,)), ((), ())), preferred_element_type=F32)


def _resident(shape):
    nd = len(shape)
    return pl.BlockSpec(shape, lambda *_: (0,) * nd, pipeline_mode=pl.Buffered(1))


def _norm_matmul_kernel(n_w, x_ref, g_ref, *refs):
    xn = _rms(x_ref[...], g_ref[...]).astype(BF16)
    for w_ref, o_ref in zip(refs[:n_w], refs[n_w:]):
        o_ref[...] = _dot(xn, w_ref[...])


def _norm_matmul(x2d, g, ws, tm=256):
    n, d = x2d.shape
    assert n % tm == 0
    wbytes = sum(w.size * 2 for w in ws)
    obytes = sum(2 * tm * w.shape[1] * 4 for w in ws)
    return pl.pallas_call(
        functools.partial(_norm_matmul_kernel, len(ws)),
        grid=(n // tm,),
        in_specs=[pl.BlockSpec((tm, d), lambda i: (i, 0)), _resident((1, d))] + [_resident(w.shape) for w in ws],
        out_specs=[pl.BlockSpec((tm, w.shape[1]), lambda i: (i, 0)) for w in ws],
        out_shape=[jax.ShapeDtypeStruct((n, w.shape[1]), F32) for w in ws],
        compiler_params=_cparams(("parallel",), wbytes + 2 * obytes + 6 * tm * d * 4 + (8 << 20)),
        name="norm_matmul",
    )(x2d, g.reshape(1, d), *ws)


SB_SUPER = 512
SB_SUB = 256
LOG2E = 1.4426950408889634


def _sb_kernel(n_past_static, hp, q_ref, kd_ref, vd_ref, kp_ref, vp_ref, o_ref, acc_s, q_s):
    bq = q_ref.shape[0]
    dh = SB_HEAD_DIM
    q_s[...] = (q_ref[...] * (dh ** -0.5 * LOG2E)).astype(BF16)

    def neg_tri(n):
        r = lax.broadcasted_iota(jnp.int32, (n, n), 0)
        c = lax.broadcasted_iota(jnp.int32, (n, n), 1)
        return jnp.where(r > c, -1.0, 0.0).astype(BF16)

    def superblock(h, kb, vb, c, masked, first):
        cols = slice(h * dh, (h + 1) * dh)
        ks = kb.shape[0]
        sub = min(SB_SUB, ks)
        nsub = ks // sub
        z = _dot_nt(q_s[:, cols], kb.astype(BF16))
        sp = jnp.maximum(z, 0.0) + jnp.log2(1.0 + jnp.exp2(-jnp.abs(z)))
        if masked:
            allowed = lax.broadcasted_iota(jnp.int32, (bq, ks), 1) < lax.broadcasted_iota(jnp.int32, (bq, ks), 0)
            sp = jnp.where(allowed, sp, 0.0)
        sp16 = sp.astype(BF16)
        stacked = jnp.concatenate([sp16[:, k * sub:(k + 1) * sub] for k in range(nsub)], axis=0)
        after = _dot(stacked, neg_tri(sub))
        ws = [None] * nsub
        for k in reversed(range(nsub)):
            kc = slice(k * sub, (k + 1) * sub)
            w = jnp.exp2(z[:, kc] - sp[:, kc] + after[k * bq:(k + 1) * bq] + c)
            if masked:
                w = jnp.where(allowed[:, kc], w, 0.0)
            ws[k] = w.astype(BF16)
            c = c - jnp.sum(sp[:, kc], axis=1, keepdims=True)
        pv = _dot(jnp.concatenate(ws, axis=1), vb.astype(BF16))
        if first:
            acc_s[:, cols] = pv
        else:
            acc_s[:, cols] += pv
        return c

    cs = tuple(superblock(h, kd_ref[:, h * dh:(h + 1) * dh], vd_ref[:, h * dh:(h + 1) * dh],
                          jnp.zeros((bq, 1), F32), True, True) for h in range(hp))

    n_past = pl.program_id(2) if n_past_static is None else n_past_static

    def body(it, cs):
        rows = pl.ds(pl.multiple_of((n_past - 1 - it) * SB_SUPER, SB_SUPER), SB_SUPER)
        if len(kp_ref.shape) == 2:
            kbs = [kp_ref[rows, h * dh:(h + 1) * dh] for h in range(hp)]
            vbs = [vp_ref[rows, h * dh:(h + 1) * dh] for h in range(hp)]
        else:
            kbs = pltpu.einshape("thd->htd", kp_ref[rows])
            vbs = pltpu.einshape("thd->htd", vp_ref[rows])
        return tuple(superblock(h, kbs[h], vbs[h], cs[h], False, False) for h in range(hp))

    lax.fori_loop(0, n_past, body, cs)
    o_ref[...] = acc_s[...]


def _sb_attention(q, kd, vd, kp, vp, prompt, hp):
    b, tq, width = q.shape
    heads = width // SB_HEAD_DIM
    tp = kp.shape[1]
    assert tp % SB_SUPER == 0 and heads % hp == 0
    if prompt:
        bq, n_past = SB_SUPER, None
    else:
        bq, n_past = tq, tp // SB_SUPER
    assert tq % bq == 0
    gw = hp * SB_HEAD_DIM
    qspec = pl.BlockSpec((None, bq, gw), lambda bi, h, i: (bi, i, h))
    if kp.ndim == 4:
        assert hp == heads
        pspec = pl.BlockSpec((None, tp, heads, SB_HEAD_DIM), lambda bi, h, i: (bi, 0, 0, 0))
    else:
        pspec = pl.BlockSpec((None, tp, gw), lambda bi, h, i: (bi, 0, h))
    return pl.pallas_call(
        functools.partial(_sb_kernel, n_past, hp),
        grid=(b, heads // hp, tq // bq),
        in_specs=[qspec, qspec, qspec, pspec, pspec],
        out_specs=qspec,
        out_shape=jax.ShapeDtypeStruct((b, tq, width), F32),
        scratch_shapes=[pltpu.VMEM((bq, gw), F32), pltpu.VMEM((bq, gw), BF16)],
        compiler_params=_cparams(("parallel", "parallel", "arbitrary"),
                                 4 * tp * gw * 4 + 10 * bq * gw * 4 + 12 * bq * SB_SUPER * 4 * hp + (8 << 20)),
        name="sb_attention",
    )(q, kd, vd, kp, vp)


def _rw_prep_kernel(rw_width, p_ref, sh_ref, mu_ref, w0_ref, a0_ref, wa2_ref, g2_ref,
                    r_o, w_o, k_o, v_o, a_o, g_o, carry):
    @pl.when(pl.program_id(1) == 0)
    def _():
        carry[...] = sh_ref[...]

    p = p_ref[...]
    tt = p.shape[0]
    row = lax.broadcasted_iota(jnp.int32, p.shape, 0)
    prev = jnp.where(row == 0, carry[...], pltpu.roll(p, 1, axis=0))
    carry[...] = p[tt - 1:tt, :]
    ps = p + (prev - p) * mu_ref[...]
    c = rw_width
    r_o[...] = ps[:, 0:c]
    k_o[...] = ps[:, c:2 * c]
    v_o[...] = ps[:, 2 * c:3 * c]
    xwa = ps[:, 3 * c:3 * c + RW_DECAY_RANK + RW_A_RANK]
    lane = lax.broadcasted_iota(jnp.int32, xwa.shape, 1)
    lowrank_in = jnp.where(lane < RW_DECAY_RANK, jnp.tanh(xwa), xwa).astype(BF16)
    wa = _dot(lowrank_in, wa2_ref[...])
    w = -_softplus(-(w0_ref[...] + wa[:, 0:c])) - 0.5
    w_o[...] = jnp.exp(-jnp.exp(w))
    a_o[...] = _sigmoid(a0_ref[...] + wa[:, c:2 * c])
    xg = ps[:, 3 * c + RW_DECAY_RANK + RW_A_RANK:]
    g_o[...] = _dot(_sigmoid(xg).astype(BF16), g2_ref[...])


def _rw_prep(p, shift0, mu, w0, a0, wa2, g2, tt):
    b, t, proj = p.shape
    c = w0.shape[-1]
    assert t % tt == 0
    tok = pl.BlockSpec((None, tt, c), lambda bi, j: (bi, j, 0))
    return pl.pallas_call(
        functools.partial(_rw_prep_kernel, c),
        grid=(b, t // tt),
        in_specs=[pl.BlockSpec((None, tt, proj), lambda bi, j: (bi, j, 0)),
                  pl.BlockSpec((None, 1, proj), lambda bi, j: (bi, 0, 0)),
                  _resident((1, proj)), _resident((1, c)), _resident((1, c)),
                  _resident(wa2.shape), _resident(g2.shape)],
        out_specs=[tok] * 6,
        out_shape=[jax.ShapeDtypeStruct((b, t, c), F32)] * 6,
        scratch_shapes=[pltpu.VMEM((1, proj), F32)],
        compiler_params=_cparams(("parallel", "arbitrary"), 2 * tt * proj * 4 * 4 + 12 * tt * c * 4 + (8 << 20)),
        name="rw_prep",
    )(p, shift0, mu.reshape(1, proj), w0.reshape(1, c), a0.reshape(1, c), wa2, g2)


def _rw_scan_kernel(r_ref, w_ref, k_ref, v_ref, a_ref, kk_p, ka_p, rk_p, lng_p, lnb_p, s0_ref,
                    o_ref, S, kk_s, b_s, kt_s, r_s):
    j = pl.program_id(1)
    nk = S.shape[0]
    tt = r_ref.shape[0]

    @pl.when(j == 0)
    def _():
        S[...] = s0_ref[...]

    def step(t, g):
        r = r_ref[t]
        k = k_ref[t]
        v = v_ref[t]
        a = a_ref[t]
        kk = k * kk_p[...]
        nrm = jnp.sqrt(jnp.sum(kk * kk, axis=0, keepdims=True))
        kk = kk / jnp.maximum(nrm, 1e-12)
        kt = k * (1.0 + (a - 1.0) * ka_p[...])
        g_new = g * w_ref[t]
        inv_g = 1.0 / g_new
        kk_s[...] = kk * g
        b_s[...] = (kk * a) * inv_g
        kt_s[...] = kt * inv_g
        r_s[...] = r * g_new

        def p1(kx, sa):
            return sa + S[kx] * kk_s[pl.ds(kx, 1), :]

        sa = lax.fori_loop(0, nk, p1, jnp.zeros(v.shape, F32), unroll=16)

        def p2(kx, y):
            row = pl.ds(kx, 1)
            s_new = S[kx] - sa * b_s[row, :] + v * kt_s[row, :]
            S[kx] = s_new
            return y + s_new * r_s[row, :]

        y = lax.fori_loop(0, nk, p2, jnp.zeros(v.shape, F32), unroll=16)
        mean = jnp.mean(y, axis=0, keepdims=True)
        d = y - mean
        var = jnp.mean(d * d, axis=0, keepdims=True)
        bonus = jnp.sum(r * kt * rk_p[...], axis=0, keepdims=True) * v
        o_ref[t] = d * lax.rsqrt(var + GN_EPS) * lng_p[...] + lnb_p[...] + bonus
        return g_new

    r_s[...] = lax.fori_loop(0, tt, step, jnp.ones(kk_s.shape, F32))

    def fold(kx, carry):
        S[kx] = S[kx] * r_s[pl.ds(kx, 1), :]
        return carry

    lax.fori_loop(0, nk, fold, 0, unroll=16)


def _rw_scan(rT, wT, kT, vT, aT, params, s0, tt):
    t, n, ctot = rT.shape
    assert t % tt == 0 and ctot % V7X_LANES == 0
    seq = pl.BlockSpec((tt, n, V7X_LANES), lambda c, j: (j, 0, c))
    par = pl.BlockSpec((n, V7X_LANES), lambda c, j: (0, c))
    st = pl.BlockSpec((n, n, V7X_LANES), lambda c, j: (0, 0, c))
    return pl.pallas_call(
        _rw_scan_kernel,
        grid=(ctot // V7X_LANES, t // tt),
        in_specs=[seq] * 5 + [par] * 5 + [st],
        out_specs=[seq, st],
        out_shape=[jax.ShapeDtypeStruct((t, n, ctot), F32), jax.ShapeDtypeStruct((n, n, ctot), F32)],
        scratch_shapes=[pltpu.VMEM((n, V7X_LANES), F32)] * 4,
        compiler_params=_cparams(("parallel", "arbitrary"), 12 * tt * n * V7X_LANES * 4 + 5 * n * n * V7X_LANES * 4 + (8 << 20)),
        name="rw_scan",
    )(rT, wT, kT, vT, aT, *params, s0)


def _mix_kernel(osb_ref, orw_ref, g_ref, gsb_ref, grw_ref, x_ref, wsb_ref, wrw_ref, wout_ref, h_ref):
    sb = _dot(osb_ref[...].astype(BF16), wsb_ref[...])
    rw = _dot((orw_ref[...] * g_ref[...]).astype(BF16), wrw_ref[...])
    mixed = _sigmoid(gsb_ref[...]) * sb + _sigmoid(grw_ref[...]) * rw
    h_ref[...] = x_ref[...] + _dot(mixed.astype(BF16), wout_ref[...])


def _mix(o_sb, o_rw, g, gate_sb, gate_rw, x2d, wsb, wrw, wout, tm=256):
    n, d = x2d.shape
    c = o_sb.shape[1]
    assert n % tm == 0
    half = pl.BlockSpec((tm, c), lambda i: (i, 0))
    full = pl.BlockSpec((tm, d), lambda i: (i, 0))
    return pl.pallas_call(
        _mix_kernel,
        grid=(n // tm,),
        in_specs=[half, half, half, full, full, full, _resident(wsb.shape), _resident(wrw.shape), _resident(wout.shape)],
        out_specs=full,
        out_shape=jax.ShapeDtypeStruct((n, d), F32),
        compiler_params=_cparams(("parallel",), (wsb.size + wrw.size + wout.size) * 2 + 2 * tm * (3 * c + 4 * d) * 4 + (12 << 20)),
        name="branch_mix",
    )(o_sb, o_rw, g, gate_sb, gate_rw, x2d, wsb, wrw, wout)


PEER_RANKS = PEER_TOPK + 1
PEER_RANK_ROWS = 24
PEER_UNRANKED = 127.0


def _top_ranks(s, t_ref, want_ranks):
    t_ref[...] = jnp.full(t_ref.shape, NEG_BIG, F32)
    ranks = jnp.full(s.shape, PEER_UNRANKED, F32) if want_ranks else None
    for rnk in range(PEER_RANKS):
        m = jnp.max(s, axis=0, keepdims=True)
        t_ref[rnk:rnk + 1, :] = m
        hit = s == m
        if want_ranks:
            ranks = jnp.where(hit, float(rnk), ranks)
        s = jnp.where(hit, NEG_BIG, s)
    return ranks


def _peer_select_kernel(h_ref, g_ref, wq_ref, k1_ref, k2_ref, xn_o, lim_o, coef_o, rank2_o, e2_o, t1_s, t2_s):
    xn = _rms(h_ref[...], g_ref[...]).astype(BF16)
    xn_o[...] = xn
    q = _dot(xn, wq_ref[...])
    k1 = k1_ref[...].astype(BF16)
    k2 = k2_ref[...].astype(BF16)
    sub = V7X_SUBLANES

    for h in range(PEER_HEADS):
        base = 2 * PEER_HALF * h
        s1 = _dot_nt(k1, q[:, base:base + PEER_HALF].astype(BF16))
        s2 = _dot_nt(k2, q[:, base + PEER_HALF:base + 2 * PEER_HALF].astype(BF16))
        _top_ranks(s1, t1_s, False)
        rank2 = _top_ranks(s2, t2_s, True)
        t1 = t1_s[...]
        t2 = t2_s[...]
        cand = jnp.concatenate(
            [t1[0:1] + t2[0:sub], t1[0:1] + t2[sub:2 * sub], t1[0:1] + t2[2 * sub:3 * sub],
             t2[0:1] + t1[0:sub], t2[0:1] + t1[sub:2 * sub], t2[0:1] + t1[2 * sub:3 * sub],
             t1[1:2] + t2[0:sub], t2[1:2] + t1[0:sub],
             t1[2:3] + t2[0:sub], t1[3:4] + t2[0:sub], t1[4:5] + t2[0:sub]], axis=0)
        top = jnp.max(cand, axis=0, keepdims=True)
        zsum = jnp.zeros(top.shape, F32)
        lo16 = top
        for rnk in range(PEER_RANKS):
            m = jnp.max(cand, axis=0, keepdims=True)
            if rnk < PEER_TOPK:
                zsum = zsum + jnp.exp(m - top)
                lo16 = m
            else:
                tau = 0.5 * (lo16 + m)
            cand = jnp.where(cand == m, NEG_BIG, cand)
        lim = jnp.zeros(s1.shape, F32)
        for a in range(PEER_TOPK):
            width = jnp.sum(jnp.where(t2 > tau - t1[a:a + 1], 1.0, 0.0), axis=0, keepdims=True)
            lim = jnp.where(s1 == t1[a:a + 1], width, lim)
        lim_o[h] = lim
        coef_o[h] = jnp.exp(s1 - t1[0:1]) / zsum
        rank2_o[h] = rank2.astype(BF16)
        e2_o[h] = jnp.exp(s2 - t2[0:1]).astype(BF16)


def _peer_select(h2d, g, wq, k1, k2, tn):
    n, d = h2d.shape
    assert n % tn == 0
    sel = pl.BlockSpec((PEER_HEADS, PEER_NKEYS, tn), lambda i: (0, 0, i))
    sel_f32 = jax.ShapeDtypeStruct((PEER_HEADS, PEER_NKEYS, n), F32)
    sel_bf16 = jax.ShapeDtypeStruct((PEER_HEADS, PEER_NKEYS, n), BF16)
    return pl.pallas_call(
        _peer_select_kernel,
        grid=(n // tn,),
        in_specs=[pl.BlockSpec((tn, d), lambda i: (i, 0)), _resident((1, d)), _resident(wq.shape),
                  _resident(k1.shape), _resident(k2.shape)],
        out_specs=[pl.BlockSpec((tn, d), lambda i: (i, 0)), sel, sel, sel, sel],
        out_shape=[jax.ShapeDtypeStruct((n, d), BF16), sel_f32, sel_f32, sel_bf16, sel_bf16],
        scratch_shapes=[pltpu.VMEM((PEER_RANK_ROWS, tn), F32)] * 2,
        compiler_params=_cparams(("parallel",), wq.size * 2 + 8 * PEER_HEADS * PEER_NKEYS * tn * 4 + 8 * tn * d * 4 + (12 << 20)),
        name="peer_select",
    )(h2d, g.reshape(1, d), wq, k1, k2)


def _peer_main_kernel(xn_ref, u_ref, v_ref, lim_ref, coef_ref, rank2_ref, e2_ref, h_ref, g_ref, y_ref, acc):
    j = pl.program_id(1)
    eb = u_ref.shape[0]
    nib = eb // PEER_NKEYS
    tile = (PEER_NKEYS, xn_ref.shape[0])

    @pl.when(j == 0)
    def _():
        acc[...] = jnp.zeros(acc.shape, F32)

    hid = _dot_nt(u_ref[...], xn_ref[...])
    parts = []
    for ii in range(nib):
        row = pl.ds(j * nib + ii, 1)
        wgt = jnp.zeros(tile, BF16)
        for h in range(PEER_HEADS):
            lim = jnp.broadcast_to(lim_ref[h, row, :], tile).astype(BF16)
            coef = jnp.broadcast_to(coef_ref[h, row, :], tile).astype(BF16)
            picked = jnp.where(rank2_ref[h] < lim, e2_ref[h], jnp.zeros((), BF16))
            wgt = wgt + coef * picked
        x = hid[ii * PEER_NKEYS:(ii + 1) * PEER_NKEYS, :].astype(BF16)
        gelu = (0.5 * x) * (1.0 + lax.erf(x * (2.0 ** -0.5)))
        parts.append(gelu * wgt)
    gated = jnp.concatenate(parts, axis=0)
    acc[...] += _dot_tn(gated, v_ref[...])

    @pl.when(j == pl.num_programs(1) - 1)
    def _():
        y_ref[...] = _rms(h_ref[...] + acc[...], g_ref[...])


def _peer_main(xn, u, v, lim, coef, rank2, e2, h2d, g, tn, eb):
    n, d = h2d.shape
    e = u.shape[0]
    assert n % tn == 0 and e % eb == 0 and eb % PEER_NKEYS == 0
    sel = pl.BlockSpec((PEER_HEADS, PEER_NKEYS, tn), lambda i, j: (0, 0, i))
    tok = pl.BlockSpec((tn, d), lambda i, j: (i, 0))
    tab = pl.BlockSpec((eb, d), lambda i, j: (j, 0))
    vmem = 2 * tn * d * 2 + 4 * eb * d * 2 + 6 * PEER_HEADS * PEER_NKEYS * tn * 4 + 5 * tn * d * 4 + (8 << 20)
    return pl.pallas_call(
        _peer_main_kernel,
        grid=(n // tn, e // eb),
        in_specs=[tok, tab, tab, sel, sel, sel, sel, tok, _resident((1, d))],
        out_specs=tok,
        out_shape=jax.ShapeDtypeStruct((n, d), F32),
        scratch_shapes=[pltpu.VMEM((tn, d), F32)],
        compiler_params=_cparams(("parallel", "arbitrary"), vmem),
        name="peer_main",
    )(xn, u, v, lim, coef, rank2, e2, h2d, g.reshape(1, d))


def _to_chain_major(x, heads):
    b, t, c = x.shape
    n = c // heads
    return x.reshape(b, t, heads, n).transpose(1, 3, 0, 2).reshape(t, n, b * heads)


def _from_chain_major(x, b):
    t, n, ch = x.shape
    heads = ch // b
    return x.reshape(t, n, b, heads).transpose(2, 0, 3, 1).reshape(b, t, heads * n)


def _head_param(p, heads, b):
    n = p.size // heads
    return jnp.tile(p.reshape(heads, n).T, (1, b))


def _layer(x, past_k, past_v, wkv0, shift0, wts, sb_hp, rw_tt, peer_tn, peer_eb):
    b, t, d = x.shape
    n_tok = b * t
    x2d = x.reshape(n_tok, d)
    sbw = wts["wq"].shape[1]
    rww = wts["w0"].shape[-1]
    rw_heads = rww // RW_HEAD_DIM
    proj = wts["w_rw"].shape[1]

    q, k, v = _norm_matmul(x2d, wts["norm1_g"], [wts["wq"], wts["wk"], wts["wv"]])
    (p_rw,) = _norm_matmul(x2d, wts["norm1_g"], [wts["w_rw"]])
    gate_sb, gate_rw = _norm_matmul(x2d, wts["norm1_g"], [wts["w_gsb"], wts["w_grw"]])

    q3, k3, v3 = (a.reshape(b, t, sbw) for a in (q, k, v))
    heads_sb = sbw // SB_HEAD_DIM
    if past_k is None:
        o_sb = _sb_attention(q3, k3, v3, k3, v3, prompt=True, hp=sb_hp)
        wkv0 = jnp.zeros((b, rw_heads, RW_HEAD_DIM, RW_HEAD_DIM), F32)
        shift0 = jnp.zeros((b, 1, proj), F32)
    else:
        o_sb = _sb_attention(q3, k3, v3, past_k, past_v, prompt=False, hp=sb_hp)

    p3 = p_rw.reshape(b, t, proj)
    r, w, kr, vr, a, g = _rw_prep(p3, shift0, wts["rw_mu"], wts["w0"], wts["a0"], wts["wa2"], wts["g2"], tt=min(t, 256))
    seqs = [_to_chain_major(s, rw_heads) for s in (r, w, kr, vr, a)]
    params = [_head_param(wts[nm], rw_heads, b) for nm in ("k_k", "k_a", "r_k", "ln_g", "ln_b")]
    s0 = wkv0.astype(F32).transpose(3, 2, 0, 1).reshape(RW_HEAD_DIM, RW_HEAD_DIM, b * rw_heads)
    oT, sT = _rw_scan(*seqs, params, s0, tt=rw_tt)
    o_rw = _from_chain_major(oT, b).reshape(n_tok, rww)
    wkv_new = sT.reshape(RW_HEAD_DIM, RW_HEAD_DIM, b, rw_heads).transpose(2, 3, 1, 0)
    shift_new = p3[:, t - 1:t, :]

    h = _mix(o_sb.reshape(n_tok, sbw), o_rw, g.reshape(n_tok, rww), gate_sb, gate_rw, x2d,
             wts["w_bsb"], wts["w_brw"], wts["w_out"])
    xn2, lim, coef, rank2, e2 = _peer_select(h, wts["norm2_g"], wts["peer_wq"], wts["peer_k1"], wts["peer_k2"], tn=256)
    y = _peer_main(xn2, wts["peer_u"], wts["peer_v"], lim, coef, rank2, e2, h, wts["final_g"], tn=peer_tn, eb=peer_eb)
    kv_shape = (b, t, heads_sb, SB_HEAD_DIM)
    return y.reshape(b, t, d), k.reshape(kv_shape), v.reshape(kv_shape), wkv_new, shift_new


def kernel(x_prompt, x_sample, cache_sb_k, cache_sb_v, state_rwkv_wkv, state_rwkv_shift, norm1_g, w_in, rw_mu, rw_w0, rw_w2, rw_a0, rw_a2, rw_g2, rw_k_k, rw_k_a, rw_r_k, rw_ln_g, rw_ln_b, w_branch_sb, w_branch_rw, w_out, norm2_g, peer_wq, peer_k1, peer_k2, peer_u, peer_v, final_norm_g):
    assert norm1_g.shape[0] == 1, "single-layer stack"
    d = x_prompt.shape[-1]
    sbw = w_branch_sb.shape[1]
    rww = w_branch_rw.shape[1]
    proj = rw_mu.shape[1]
    w = w_in[0]
    o = 0
    cols = {}
    for nm, width in (("wq", sbw), ("wk", sbw), ("wv", sbw), ("w_rw", proj), ("w_gsb", d), ("w_grw", d)):
        cols[nm] = w[:, o:o + width].astype(BF16)
        o += width
    zeros = jnp.zeros((RW_DECAY_RANK, rww), F32)
    wa2 = jnp.concatenate([jnp.concatenate([rw_w2[0], zeros], axis=1),
                           jnp.concatenate([zeros, rw_a2[0]], axis=1)], axis=0).astype(BF16)
    wts = dict(cols, norm1_g=norm1_g[0], rw_mu=rw_mu[0], w0=rw_w0[0], a0=rw_a0[0], wa2=wa2, g2=rw_g2[0].astype(BF16),
               k_k=rw_k_k[0], k_a=rw_k_a[0], r_k=rw_r_k[0].reshape(-1), ln_g=rw_ln_g[0], ln_b=rw_ln_b[0],
               w_bsb=w_branch_sb[0].astype(BF16), w_brw=w_branch_rw[0].astype(BF16), w_out=w_out[0].astype(BF16),
               norm2_g=norm2_g[0], peer_wq=peer_wq[0].astype(BF16), peer_k1=peer_k1[0], peer_k2=peer_k2[0],
               peer_u=peer_u[0].astype(BF16), peer_v=peer_v[0].astype(BF16), final_g=final_norm_g)

    sb_heads = sbw // SB_HEAD_DIM
    yp, kp, vp, wp, sp = _layer(x_prompt, None, None, None, None, wts, sb_hp=4, rw_tt=32, peer_tn=512, peer_eb=1024)
    ys, ks, vs, ws, ss = _layer(x_sample, cache_sb_k[0], cache_sb_v[0], state_rwkv_wkv[0], state_rwkv_shift[0], wts,
                                sb_hp=sb_heads, rw_tt=32, peer_tn=512, peer_eb=1024)
    st = lambda t: t[None]
    return (yp, ys, st(kp), st(vp), st(wp.astype(x_prompt.dtype)), st(sp), st(ks), st(vs), st(ws.astype(x_sample.dtype)), st(ss))
```

```python
import functools

import jax
import jax.numpy as jnp
from jax import lax
from jax.experimental import pallas as pl
from jax.experimental.pallas import tpu as pltpu

F32 = jnp.float32
BF16 = jnp.bfloat16

RMS_EPS = 1e-6
GN_EPS = 64e-5
SB_HEAD_DIM = 128
RW_HEAD_DIM = 64
RW_DECAY_RANK = 64
RW_A_RANK = 64
RW_GATE_RANK = 128
PEER_HEADS = 8
PEER_NKEYS = 128
PEER_HALF = 128
PEER_TOPK = 16

V7X_LANES = 128
V7X_SUBLANES = 8
V7X_VMEM_LIMIT_BYTES = 60000 * 1024

NEG_BIG = -1e30


def _cparams(semantics, vmem_bytes):
    return pltpu.CompilerParams(dimension_semantics=semantics,
                                vmem_limit_bytes=int(min(V7X_VMEM_LIMIT_BYTES, vmem_bytes)))


def _sigmoid(x):
    return 1.0 / (1.0 + jnp.exp(-x))


def _softplus(x):
    return jnp.maximum(x, 0.0) + jnp.log1p(jnp.exp(-jnp.abs(x)))


def _rms(x, g):
    return x * lax.rsqrt(jnp.mean(x * x, axis=-1, keepdims=True) + RMS_EPS) * g


def _dot(a, b):
    return jnp.dot(a, b, preferred_element_type=F32)


def _dot_nt(a, b):
    return lax.dot_general(a, b, (((1,), (1,)), ((), ())), preferred_element_type=F32)


def _dot_tn(a, b):
    return lax.dot_general(a, b, (((0,), (0,)), ((), ())), preferred_element_type=F32)


def _resident(shape):
    nd = len(shape)
    return pl.BlockSpec(shape, lambda *_: (0,) * nd, pipeline_mode=pl.Buffered(1))


def _norm_matmul_kernel(n_w, x_ref, g_ref, *refs):
    xn = _rms(x_ref[...], g_ref[...]).astype(BF16)
    for w_ref, o_ref in zip(refs[:n_w], refs[n_w:]):
        o_ref[...] = _dot(xn, w_ref[...])


def _norm_matmul(x2d, g, ws, tm=256):
    n, d = x2d.shape
    assert n % tm == 0
    wbytes = sum(w.size * 2 for w in ws)
    obytes = sum(2 * tm * w.shape[1] * 4 for w in ws)
    return pl.pallas_call(
        functools.partial(_norm_matmul_kernel, len(ws)),
        grid=(n // tm,),
        in_specs=[pl.BlockSpec((tm, d), lambda i: (i, 0)), _resident((1, d))] + [_resident(w.shape) for w in ws],
        out_specs=[pl.BlockSpec((tm, w.shape[1]), lambda i: (i, 0)) for w in ws],
        out_shape=[jax.ShapeDtypeStruct((n, w.shape[1]), F32) for w in ws],
        compiler_params=_cparams(("parallel",), wbytes + 2 * obytes + 6 * tm * d * 4 + (8 << 20)),
        name="norm_matmul",
    )(x2d, g.reshape(1, d), *ws)


SB_SUPER = 512
SB_SUB = 256
LOG2E = 1.4426950408889634


def _sb_kernel(n_past_static, hp, q_ref, kd_ref, vd_ref, kp_ref, vp_ref, o_ref, acc_s, q_s):
    bq = q_ref.shape[0]
    dh = SB_HEAD_DIM
    q_s[...] = (q_ref[...] * (dh ** -0.5 * LOG2E)).astype(BF16)

    def neg_tri(n):
        r = lax.broadcasted_iota(jnp.int32, (n, n), 0)
        c = lax.broadcasted_iota(jnp.int32, (n, n), 1)
        return jnp.where(r > c, -1.0, 0.0).astype(BF16)

    def superblock(h, kb, vb, c, masked, first):
        cols = slice(h * dh, (h + 1) * dh)
        ks = kb.shape[0]
        sub = min(SB_SUB, ks)
        nsub = ks // sub
        z = _dot_nt(q_s[:, cols], kb.astype(BF16))
        sp = jnp.maximum(z, 0.0) + jnp.log2(1.0 + jnp.exp2(-jnp.abs(z)))
        if masked:
            allowed = lax.broadcasted_iota(jnp.int32, (bq, ks), 1) < lax.broadcasted_iota(jnp.int32, (bq, ks), 0)
            sp = jnp.where(allowed, sp, 0.0)
        sp16 = sp.astype(BF16)
        stacked = jnp.concatenate([sp16[:, k * sub:(k + 1) * sub] for k in range(nsub)], axis=0)
        after = _dot(stacked, neg_tri(sub))
        ws = [None] * nsub
        for k in reversed(range(nsub)):
            kc = slice(k * sub, (k + 1) * sub)
            w = jnp.exp2(z[:, kc] - sp[:, kc] + after[k * bq:(k + 1) * bq] + c)
            if masked:
                w = jnp.where(allowed[:, kc], w, 0.0)
            ws[k] = w.astype(BF16)
            c = c - jnp.sum(sp[:, kc], axis=1, keepdims=True)
        pv = _dot(jnp.concatenate(ws, axis=1), vb.astype(BF16))
        if first:
            acc_s[:, cols] = pv
        else:
            acc_s[:, cols] += pv
        return c

    cs = tuple(superblock(h, kd_ref[:, h * dh:(h + 1) * dh], vd_ref[:, h * dh:(h + 1) * dh],
                          jnp.zeros((bq, 1), F32), True, True) for h in range(hp))

    n_past = pl.program_id(2) if n_past_static is None else n_past_static

    def body(it, cs):
        rows = pl.ds(pl.multiple_of((n_past - 1 - it) * SB_SUPER, SB_SUPER), SB_SUPER)
        if len(kp_ref.shape) == 2:
            kbs = [kp_ref[rows, h * dh:(h + 1) * dh] for h in range(hp)]
            vbs = [vp_ref[rows, h * dh:(h + 1) * dh] for h in range(hp)]
        else:
            kbs = pltpu.einshape("thd->htd", kp_ref[rows])
            vbs = pltpu.einshape("thd->htd", vp_ref[rows])
        return tuple(superblock(h, kbs[h], vbs[h], cs[h], False, False) for h in range(hp))

    lax.fori_loop(0, n_past, body, cs)
    o_ref[...] = acc_s[...]


def _sb_attention(q, kd, vd, kp, vp, prompt, hp):
    b, tq, width = q.shape
    heads = width // SB_HEAD_DIM
    tp = kp.shape[1]
    assert tp % SB_SUPER == 0 and heads % hp == 0
    if prompt:
        bq, n_past = SB_SUPER, None
    else:
        bq, n_past = tq, tp // SB_SUPER
    assert tq % bq == 0
    gw = hp * SB_HEAD_DIM
    qspec = pl.BlockSpec((None, bq, gw), lambda bi, h, i: (bi, i, h))
    if kp.ndim == 4:
        assert hp == heads
        pspec = pl.BlockSpec((None, tp, heads, SB_HEAD_DIM), lambda bi, h, i: (bi, 0, 0, 0))
    else:
        pspec = pl.BlockSpec((None, tp, gw), lambda bi, h, i: (bi, 0, h))
    return pl.pallas_call(
        functools.partial(_sb_kernel, n_past, hp),
        grid=(b, heads // hp, tq // bq),
        in_specs=[qspec, qspec, qspec, pspec, pspec],
        out_specs=qspec,
        out_shape=jax.ShapeDtypeStruct((b, tq, width), F32),
        scratch_shapes=[pltpu.VMEM((bq, gw), F32), pltpu.VMEM((bq, gw), BF16)],
        compiler_params=_cparams(("parallel", "parallel", "arbitrary"),
                                 4 * tp * gw * 4 + 10 * bq * gw * 4 + 12 * bq * SB_SUPER * 4 * hp + (8 << 20)),
        name="sb_attention",
    )(q, kd, vd, kp, vp)


def _rw_project_kernel(tiles_per_seq, rw_width, x_ref, g_ref, wrw_ref, sh_ref, mu_ref, w0_ref, a0_ref, wa2_ref, g2_ref,
                       r_o, w_o, k_o, v_o, a_o, g_o, last_o, carry):
    @pl.when(pl.program_id(0) % tiles_per_seq == 0)
    def _():
        carry[...] = sh_ref[...]

    p = _dot(_rms(x_ref[...], g_ref[...]).astype(BF16), wrw_ref[...])
    tm = p.shape[0]
    row = lax.broadcasted_iota(jnp.int32, p.shape, 0)
    prev = jnp.where(row == 0, carry[...], pltpu.roll(p, 1, axis=0))
    carry[...] = p[tm - 1:tm, :]
    last_o[...] = p[tm - 1:tm, :]
    ps = p + (prev - p) * mu_ref[...]
    c = rw_width
    r_o[...] = ps[:, 0:c]
    k_o[...] = ps[:, c:2 * c]
    v_o[...] = ps[:, 2 * c:3 * c]
    xwa = ps[:, 3 * c:3 * c + RW_DECAY_RANK + RW_A_RANK]
    lane = lax.broadcasted_iota(jnp.int32, xwa.shape, 1)
    lowrank_in = jnp.where(lane < RW_DECAY_RANK, jnp.tanh(xwa), xwa).astype(BF16)
    wa = _dot(lowrank_in, wa2_ref[...])
    w = -_softplus(-(w0_ref[...] + wa[:, 0:c])) - 0.5
    w_o[...] = jnp.exp(-jnp.exp(w))
    a_o[...] = _sigmoid(a0_ref[...] + wa[:, c:2 * c])
    xg = ps[:, 3 * c + RW_DECAY_RANK + RW_A_RANK:]
    g_o[...] = _dot(_sigmoid(xg).astype(BF16), g2_ref[...])


def _rw_project(x2d, norm_g, w_rw, shift0, mu, w0, a0, wa2, g2, seq_len, tm):
    n, d = x2d.shape
    proj = w_rw.shape[1]
    c = w0.shape[-1]
    assert seq_len % tm == 0 and n % seq_len == 0
    tps = seq_len // tm
    tok = pl.BlockSpec((tm, c), lambda i: (i, 0))
    per_seq = pl.BlockSpec((None, 1, proj), lambda i: (i // tps, 0, 0))
    return pl.pallas_call(
        functools.partial(_rw_project_kernel, tps, c),
        grid=(n // tm,),
        in_specs=[pl.BlockSpec((tm, d), lambda i: (i, 0)), _resident((1, d)), _resident(w_rw.shape), per_seq,
                  _resident((1, proj)), _resident((1, c)), _resident((1, c)), _resident(wa2.shape), _resident(g2.shape)],
        out_specs=[tok] * 6 + [per_seq],
        out_shape=[jax.ShapeDtypeStruct((n, c), F32)] * 6 + [jax.ShapeDtypeStruct((n // seq_len, 1, proj), F32)],
        scratch_shapes=[pltpu.VMEM((1, proj), F32)],
        compiler_params=_cparams(("arbitrary",), w_rw.size * 2 + 6 * tm * proj * 4 + 14 * tm * c * 4 + 4 * tm * d * 4 + (8 << 20)),
        name="rw_project",
    )(x2d, norm_g.reshape(1, d), w_rw, shift0, mu.reshape(1, proj), w0.reshape(1, c), a0.reshape(1, c), wa2, g2)


def _rw_scan_kernel(r_ref, w_ref, k_ref, v_ref, a_ref, kk_p, ka_p, rk_p, lng_p, lnb_p, s0_ref,
                    o_ref, S, kk_s, b_s, kt_s, r_s):
    j = pl.program_id(1)
    nk = S.shape[0]
    tt = r_ref.shape[0]

    @pl.when(j == 0)
    def _():
        S[...] = s0_ref[...]

    def step(t, g):
        r = r_ref[t]
        k = k_ref[t]
        v = v_ref[t]
        a = a_ref[t]
        kk = k * kk_p[...]
        nrm = jnp.sqrt(jnp.sum(kk * kk, axis=0, keepdims=True))
        kk = kk / jnp.maximum(nrm, 1e-12)
        kt = k * (1.0 + (a - 1.0) * ka_p[...])
        g_new = g * w_ref[t]
        inv_g = 1.0 / g_new
        kk_s[...] = kk * g
        b_s[...] = (kk * a) * inv_g
        kt_s[...] = kt * inv_g
        r_s[...] = r * g_new

        def p1(kx, sa):
            return sa + S[kx] * kk_s[pl.ds(kx, 1), :]

        sa = lax.fori_loop(0, nk, p1, jnp.zeros(v.shape, F32), unroll=16)

        def p2(kx, y):
            row = pl.ds(kx, 1)
            s_new = S[kx] - sa * b_s[row, :] + v * kt_s[row, :]
            S[kx] = s_new
            return y + s_new * r_s[row, :]

        y = lax.fori_loop(0, nk, p2, jnp.zeros(v.shape, F32), unroll=16)
        mean = jnp.mean(y, axis=0, keepdims=True)
        d = y - mean
        var = jnp.mean(d * d, axis=0, keepdims=True)
        bonus = jnp.sum(r * kt * rk_p[...], axis=0, keepdims=True) * v
        o_ref[t] = d * lax.rsqrt(var + GN_EPS) * lng_p[...] + lnb_p[...] + bonus
        return g_new

    r_s[...] = lax.fori_loop(0, tt, step, jnp.ones(kk_s.shape, F32))

    def fold(kx, carry):
        S[kx] = S[kx] * r_s[pl.ds(kx, 1), :]
        return carry

    lax.fori_loop(0, nk, fold, 0, unroll=16)


def _rw_scan(rT, wT, kT, vT, aT, params, s0, tt):
    t, n, ctot = rT.shape
    assert t % tt == 0 and ctot % V7X_LANES == 0
    seq = pl.BlockSpec((tt, n, V7X_LANES), lambda c, j: (j, 0, c))
    par = pl.BlockSpec((n, V7X_LANES), lambda c, j: (0, c))
    st = pl.BlockSpec((n, n, V7X_LANES), lambda c, j: (0, 0, c))
    return pl.pallas_call(
        _rw_scan_kernel,
        grid=(ctot // V7X_LANES, t // tt),
        in_specs=[seq] * 5 + [par] * 5 + [st],
        out_specs=[seq, st],
        out_shape=[jax.ShapeDtypeStruct((t, n, ctot), F32), jax.ShapeDtypeStruct((n, n, ctot), F32)],
        scratch_shapes=[pltpu.VMEM((n, V7X_LANES), F32)] * 4,
        compiler_params=_cparams(("parallel", "arbitrary"), 12 * tt * n * V7X_LANES * 4 + 5 * n * n * V7X_LANES * 4 + (8 << 20)),
        name="rw_scan",
    )(rT, wT, kT, vT, aT, *params, s0)


def _mix_kernel(osb_ref, orw_ref, g_ref, gsb_ref, grw_ref, x_ref, wsb_ref, wrw_ref, wout_ref, h_ref):
    sb = _dot(osb_ref[...].astype(BF16), wsb_ref[...])
    rw = _dot((orw_ref[...] * g_ref[...]).astype(BF16), wrw_ref[...])
    mixed = _sigmoid(gsb_ref[...]) * sb + _sigmoid(grw_ref[...]) * rw
    h_ref[...] = x_ref[...] + _dot(mixed.astype(BF16), wout_ref[...])


def _mix(o_sb, o_rw, g, gate_sb, gate_rw, x2d, wsb, wrw, wout, tm=256):
    n, d = x2d.shape
    c = o_sb.shape[1]
    assert n % tm == 0
    half = pl.BlockSpec((tm, c), lambda i: (i, 0))
    full = pl.BlockSpec((tm, d), lambda i: (i, 0))
    return pl.pallas_call(
        _mix_kernel,
        grid=(n // tm,),
        in_specs=[half, half, half, full, full, full, _resident(wsb.shape), _resident(wrw.shape), _resident(wout.shape)],
        out_specs=full,
        out_shape=jax.ShapeDtypeStruct((n, d), F32),
        compiler_params=_cparams(("parallel",), (wsb.size + wrw.size + wout.size) * 2 + 2 * tm * (3 * c + 4 * d) * 4 + (12 << 20)),
        name="branch_mix",
    )(o_sb, o_rw, g, gate_sb, gate_rw, x2d, wsb, wrw, wout)


PEER_RANKS = PEER_TOPK + 1
PEER_RANK_ROWS = 24
PEER_UNRANKED = 127.0


def _top_ranks(s, t_ref, want_ranks):
    t_ref[...] = jnp.full(t_ref.shape, NEG_BIG, F32)
    ranks = jnp.full(s.shape, PEER_UNRANKED, F32) if want_ranks else None
    for rnk in range(PEER_RANKS):
        m = jnp.max(s, axis=0, keepdims=True)
        t_ref[rnk:rnk + 1, :] = m
        hit = s == m
        if want_ranks:
            ranks = jnp.where(hit, float(rnk), ranks)
        s = jnp.where(hit, NEG_BIG, s)
    return ranks


def _peer_select_kernel(h_ref, g_ref, wq_ref, k1_ref, k2_ref, xn_o, lim_o, coef_o, rank2_o, e2_o, t1_s, t2_s):
    xn = _rms(h_ref[...], g_ref[...]).astype(BF16)
    xn_o[...] = xn
    q = _dot(xn, wq_ref[...])
    k1 = k1_ref[...].astype(BF16)
    k2 = k2_ref[...].astype(BF16)
    sub = V7X_SUBLANES

    for h in range(PEER_HEADS):
        base = 2 * PEER_HALF * h
        s1 = _dot_nt(k1, q[:, base:base + PEER_HALF].astype(BF16))
        s2 = _dot_nt(k2, q[:, base + PEER_HALF:base + 2 * PEER_HALF].astype(BF16))
        _top_ranks(s1, t1_s, False)
        rank2 = _top_ranks(s2, t2_s, True)
        t1 = t1_s[...]
        t2 = t2_s[...]
        cand = jnp.concatenate(
            [t1[0:1] + t2[0:sub], t1[0:1] + t2[sub:2 * sub], t1[0:1] + t2[2 * sub:3 * sub],
             t2[0:1] + t1[0:sub], t2[0:1] + t1[sub:2 * sub], t2[0:1] + t1[2 * sub:3 * sub],
             t1[1:2] + t2[0:sub], t2[1:2] + t1[0:sub],
             t1[2:3] + t2[0:sub], t1[3:4] + t2[0:sub], t1[4:5] + t2[0:sub]], axis=0)
        top = jnp.max(cand, axis=0, keepdims=True)
        zsum = jnp.zeros(top.shape, F32)
        lo16 = top
        for rnk in range(PEER_RANKS):
            m = jnp.max(cand, axis=0, keepdims=True)
            if rnk < PEER_TOPK:
                zsum = zsum + jnp.exp(m - top)
                lo16 = m
            else:
                tau = 0.5 * (lo16 + m)
            cand = jnp.where(cand == m, NEG_BIG, cand)
        lim = jnp.zeros(s1.shape, F32)
        for a in range(PEER_TOPK):
            width = jnp.sum(jnp.where(t2 > tau - t1[a:a + 1], 1.0, 0.0), axis=0, keepdims=True)
            lim = jnp.where(s1 == t1[a:a + 1], width, lim)
        lim_o[h] = lim
        coef_o[h] = jnp.exp(s1 - t1[0:1]) / zsum
        rank2_o[h] = rank2.astype(BF16)
        e2_o[h] = jnp.exp(s2 - t2[0:1]).astype(BF16)


def _peer_select(h2d, g, wq, k1, k2, tn):
    n, d = h2d.shape
    assert n % tn == 0
    sel = pl.BlockSpec((PEER_HEADS, PEER_NKEYS, tn), lambda i: (0, 0, i))
    sel_f32 = jax.ShapeDtypeStruct((PEER_HEADS, PEER_NKEYS, n), F32)
    sel_bf16 = jax.ShapeDtypeStruct((PEER_HEADS, PEER_NKEYS, n), BF16)
    return pl.pallas_call(
        _peer_select_kernel,
        grid=(n // tn,),
        in_specs=[pl.BlockSpec((tn, d), lambda i: (i, 0)), _resident((1, d)), _resident(wq.shape),
                  _resident(k1.shape), _resident(k2.shape)],
        out_specs=[pl.BlockSpec((tn, d), lambda i: (i, 0)), sel, sel, sel, sel],
        out_shape=[jax.ShapeDtypeStruct((n, d), BF16), sel_f32, sel_f32, sel_bf16, sel_bf16],
        scratch_shapes=[pltpu.VMEM((PEER_RANK_ROWS, tn), F32)] * 2,
        compiler_params=_cparams(("parallel",), wq.size * 2 + 8 * PEER_HEADS * PEER_NKEYS * tn * 4 + 8 * tn * d * 4 + (12 << 20)),
        name="peer_select",
    )(h2d, g.reshape(1, d), wq, k1, k2)


def _peer_main_kernel(xn_ref, u_ref, v_ref, lim_ref, coef_ref, rank2_ref, e2_ref, h_ref, g_ref, y_ref, acc):
    j = pl.program_id(1)
    eb = u_ref.shape[0]
    nib = eb // PEER_NKEYS
    tile = (PEER_NKEYS, xn_ref.shape[0])

    @pl.when(j == 0)
    def _():
        acc[...] = jnp.zeros(acc.shape, F32)

    hid = _dot_nt(u_ref[...], xn_ref[...])
    parts = []
    for ii in range(nib):
        row = pl.ds(j * nib + ii, 1)
        wgt = jnp.zeros(tile, BF16)
        for h in range(PEER_HEADS):
            lim = jnp.broadcast_to(lim_ref[h, row, :], tile).astype(BF16)
            coef = jnp.broadcast_to(coef_ref[h, row, :], tile).astype(BF16)
            picked = jnp.where(rank2_ref[h] < lim, e2_ref[h], jnp.zeros((), BF16))
            wgt = wgt + coef * picked
        x = hid[ii * PEER_NKEYS:(ii + 1) * PEER_NKEYS, :].astype(BF16)
        gelu = (0.5 * x) * (1.0 + lax.erf(x * (2.0 ** -0.5)))
        parts.append(gelu * wgt)
    gated = jnp.concatenate(parts, axis=0)
    acc[...] += _dot_tn(gated, v_ref[...])

    @pl.when(j == pl.num_programs(1) - 1)
    def _():
        y_ref[...] = _rms(h_ref[...] + acc[...], g_ref[...])


def _peer_main(xn, u, v, lim, coef, rank2, e2, h2d, g, tn, eb):
    n, d = h2d.shape
    e = u.shape[0]
    assert n % tn == 0 and e % eb == 0 and eb % PEER_NKEYS == 0
    sel = pl.BlockSpec((PEER_HEADS, PEER_NKEYS, tn), lambda i, j: (0, 0, i))
    tok = pl.BlockSpec((tn, d), lambda i, j: (i, 0))
    tab = pl.BlockSpec((eb, d), lambda i, j: (j, 0))
    vmem = 2 * tn * d * 2 + 4 * eb * d * 2 + 6 * PEER_HEADS * PEER_NKEYS * tn * 4 + 5 * tn * d * 4 + (8 << 20)
    return pl.pallas_call(
        _peer_main_kernel,
        grid=(n // tn, e // eb),
        in_specs=[tok, tab, tab, sel, sel, sel, sel, tok, _resident((1, d))],
        out_specs=tok,
        out_shape=jax.ShapeDtypeStruct((n, d), F32),
        scratch_shapes=[pltpu.VMEM((tn, d), F32)],
        compiler_params=_cparams(("parallel", "arbitrary"), vmem),
        name="peer_main",
    )(xn, u, v, lim, coef, rank2, e2, h2d, g.reshape(1, d))


def _to_chain_major(x, heads):
    b, t, c = x.shape
    n = c // heads
    return x.reshape(b, t, heads, n).transpose(1, 3, 0, 2).reshape(t, n, b * heads)


def _from_chain_major(x, b):
    t, n, ch = x.shape
    heads = ch // b
    return x.reshape(t, n, b, heads).transpose(2, 0, 3, 1).reshape(b, t, heads * n)


def _head_param(p, heads, b):
    n = p.size // heads
    return jnp.tile(p.reshape(heads, n).T, (1, b))


def _layer(x, past_k, past_v, wkv0, shift0, wts, sb_hp, rw_tt, peer_tn, peer_eb):
    b, t, d = x.shape
    n_tok = b * t
    x2d = x.reshape(n_tok, d)
    sbw = wts["wq"].shape[1]
    rww = wts["w0"].shape[-1]
    rw_heads = rww // RW_HEAD_DIM
    proj = wts["w_rw"].shape[1]

    q, k, v, gate_sb, gate_rw = _norm_matmul(
        x2d, wts["norm1_g"], [wts["wq"], wts["wk"], wts["wv"], wts["w_gsb"], wts["w_grw"]])

    q3, k3, v3 = (a.reshape(b, t, sbw) for a in (q, k, v))
    heads_sb = sbw // SB_HEAD_DIM
    if past_k is None:
        o_sb = _sb_attention(q3, k3, v3, k3, v3, prompt=True, hp=sb_hp)
        wkv0 = jnp.zeros((b, rw_heads, RW_HEAD_DIM, RW_HEAD_DIM), F32)
        shift0 = jnp.zeros((b, 1, proj), F32)
    else:
        o_sb = _sb_attention(q3, k3, v3, past_k, past_v, prompt=False, hp=sb_hp)

    r, w, kr, vr, a, g, shift_new = _rw_project(x2d, wts["norm1_g"], wts["w_rw"], shift0, wts["rw_mu"], wts["w0"],
                                                wts["a0"], wts["wa2"], wts["g2"], seq_len=t, tm=min(t, 256))
    seqs = [_to_chain_major(s.reshape(b, t, rww), rw_heads) for s in (r, w, kr, vr, a)]
    params = [_head_param(wts[nm], rw_heads, b) for nm in ("k_k", "k_a", "r_k", "ln_g", "ln_b")]
    s0 = wkv0.astype(F32).transpose(3, 2, 0, 1).reshape(RW_HEAD_DIM, RW_HEAD_DIM, b * rw_heads)
    oT, sT = _rw_scan(*seqs, params, s0, tt=rw_tt)
    o_rw = _from_chain_major(oT, b).reshape(n_tok, rww)
    wkv_new = sT.reshape(RW_HEAD_DIM, RW_HEAD_DIM, b, rw_heads).transpose(2, 3, 1, 0)

    h = _mix(o_sb.reshape(n_tok, sbw), o_rw, g, gate_sb, gate_rw, x2d,
             wts["w_bsb"], wts["w_brw"], wts["w_out"])
    xn2, lim, coef, rank2, e2 = _peer_select(h, wts["norm2_g"], wts["peer_wq"], wts["peer_k1"], wts["peer_k2"], tn=256)
    y = _peer_main(xn2, wts["peer_u"], wts["peer_v"], lim, coef, rank2, e2, h, wts["final_g"], tn=peer_tn, eb=peer_eb)
    kv_shape = (b, t, heads_sb, SB_HEAD_DIM)
    return y.reshape(b, t, d), k.reshape(kv_shape), v.reshape(kv_shape), wkv_new, shift_new


def kernel(x_prompt, x_sample, cache_sb_k, cache_sb_v, state_rwkv_wkv, state_rwkv_shift, norm1_g, w_in, rw_mu, rw_w0, rw_w2, rw_a0, rw_a2, rw_g2, rw_k_k, rw_k_a, rw_r_k, rw_ln_g, rw_ln_b, w_branch_sb, w_branch_rw, w_out, norm2_g, peer_wq, peer_k1, peer_k2, peer_u, peer_v, final_norm_g):
    assert norm1_g.shape[0] == 1, "single-layer stack"
    d = x_prompt.shape[-1]
    sbw = w_branch_sb.shape[1]
    rww = w_branch_rw.shape[1]
    proj = rw_mu.shape[1]
    w = w_in[0]
    o = 0
    cols = {}
    for nm, width in (("wq", sbw), ("wk", sbw), ("wv", sbw), ("w_rw", proj), ("w_gsb", d), ("w_grw", d)):
        cols[nm] = w[:, o:o + width].astype(BF16)
        o += width
    zeros = jnp.zeros((RW_DECAY_RANK, rww), F32)
    wa2 = jnp.concatenate([jnp.concatenate([rw_w2[0], zeros], axis=1),
                           jnp.concatenate([zeros, rw_a2[0]], axis=1)], axis=0).astype(BF16)
    wts = dict(cols, norm1_g=norm1_g[0], rw_mu=rw_mu[0], w0=rw_w0[0], a0=rw_a0[0], wa2=wa2, g2=rw_g2[0].astype(BF16),
               k_k=rw_k_k[0], k_a=rw_k_a[0], r_k=rw_r_k[0].reshape(-1), ln_g=rw_ln_g[0], ln_b=rw_ln_b[0],
               w_bsb=w_branch_sb[0].astype(BF16), w_brw=w_branch_rw[0].astype(BF16), w_out=w_out[0].astype(BF16),
               norm2_g=norm2_g[0], peer_wq=peer_wq[0].astype(BF16), peer_k1=peer_k1[0], peer_k2=peer_k2[0],
               peer_u=peer_u[0].astype(BF16), peer_v=peer_v[0].astype(BF16), final_g=final_norm_g)

    sb_heads = sbw // SB_HEAD_DIM
    yp, kp, vp, wp, sp = _layer(x_prompt, None, None, None, None, wts, sb_hp=4, rw_tt=32, peer_tn=512, peer_eb=1024)
    ys, ks, vs, ws, ss = _layer(x_sample, cache_sb_k[0], cache_sb_v[0], state_rwkv_wkv[0], state_rwkv_shift[0], wts,
                                sb_hp=sb_heads, rw_tt=32, peer_tn=512, peer_eb=1024)
    st = lambda t: t[None]
    return (yp, ys, st(kp), st(vp), st(wp.astype(x_prompt.dtype)), st(sp), st(ks), st(vs), st(ws.astype(x_sample.dtype)), st(ss))
```

```python
import functools

import jax
import jax.numpy as jnp
from jax import lax
from jax.experimental import pallas as pl
from jax.experimental.pallas import tpu as pltpu

F32 = jnp.float32
BF16 = jnp.bfloat16

RMS_EPS = 1e-6
GN_EPS = 64e-5
SB_HEAD_DIM = 128
RW_HEAD_DIM = 64
RW_DECAY_RANK = 64
RW_A_RANK = 64
RW_GATE_RANK = 128
PEER_HEADS = 8
PEER_NKEYS = 128
PEER_HALF = 128
PEER_TOPK = 16

V7X_LANES = 128
V7X_SUBLANES = 8
V7X_VMEM_LIMIT_BYTES = 60000 * 1024

NEG_BIG = -1e30


def _cparams(semantics, vmem_bytes):
    return pltpu.CompilerParams(dimension_semantics=semantics,
                                vmem_limit_bytes=int(min(V7X_VMEM_LIMIT_BYTES, vmem_bytes)))


def _sigmoid(x):
    return 1.0 / (1.0 + jnp.exp(-x))


def _softplus(x):
    return jnp.maximum(x, 0.0) + jnp.log1p(jnp.exp(-jnp.abs(x)))


def _rms(x, g):
    return x * lax.rsqrt(jnp.mean(x * x, axis=-1, keepdims=True) + RMS_EPS) * g


def _dot(a, b):
    return jnp.dot(a, b, preferred_element_type=F32)


def _dot_nt(a, b):
    return lax.dot_general(a, b, (((1,), (1,)), ((), ())), preferred_element_type=F32)


def _dot_tn(a, b):
    return lax.dot_general(a, b, (((0,), (0,)), ((), ())), preferred_element_type=F32)


def _resident(shape):
    nd = len(shape)
    return pl.BlockSpec(shape, lambda *_: (0,) * nd, pipeline_mode=pl.Buffered(1))


def _norm_matmul_kernel(n_w, x_ref, g_ref, *refs):
    xn = _rms(x_ref[...], g_ref[...]).astype(BF16)
    for w_ref, o_ref in zip(refs[:n_w], refs[n_w:]):
        o_ref[...] = _dot(xn, w_ref[...])


def _norm_matmul(x2d, g, ws, tm=256):
    n, d = x2d.shape
    assert n % tm == 0
    wbytes = sum(w.size * 2 for w in ws)
    obytes = sum(2 * tm * w.shape[1] * 4 for w in ws)
    return pl.pallas_call(
        functools.partial(_norm_matmul_kernel, len(ws)),
        grid=(n // tm,),
        in_specs=[pl.BlockSpec((tm, d), lambda i: (i, 0)), _resident((1, d))] + [_resident(w.shape) for w in ws],
        out_specs=[pl.BlockSpec((tm, w.shape[1]), lambda i: (i, 0)) for w in ws],
        out_shape=[jax.ShapeDtypeStruct((n, w.shape[1]), F32) for w in ws],
        compiler_params=_cparams(("parallel",), wbytes + 2 * obytes + 6 * tm * d * 4 + (8 << 20)),
        name="norm_matmul",
    )(x2d, g.reshape(1, d), *ws)


SB_SUPER = 512
SB_SUB = 256
LOG2E = 1.4426950408889634


def _sb_kernel(n_past_static, hp, q_ref, kd_ref, vd_ref, kp_ref, vp_ref, o_ref, acc_s, q_s):
    bq = q_ref.shape[0]
    dh = SB_HEAD_DIM
    q_s[...] = (q_ref[...] * (dh ** -0.5 * LOG2E)).astype(BF16)

    def neg_tri(n):
        r = lax.broadcasted_iota(jnp.int32, (n, n), 0)
        c = lax.broadcasted_iota(jnp.int32, (n, n), 1)
        return jnp.where(r > c, -1.0, 0.0).astype(BF16)

    def superblock(h, kb, vb, c, masked, first):
        cols = slice(h * dh, (h + 1) * dh)
        ks = kb.shape[0]
        sub = min(SB_SUB, ks)
        nsub = ks // sub
        z = _dot_nt(q_s[:, cols], kb.astype(BF16))
        sp = jnp.maximum(z, 0.0) + jnp.log2(1.0 + jnp.exp2(-jnp.abs(z)))
        if masked:
            allowed = lax.broadcasted_iota(jnp.int32, (bq, ks), 1) < lax.broadcasted_iota(jnp.int32, (bq, ks), 0)
            sp = jnp.where(allowed, sp, 0.0)
        sp16 = sp.astype(BF16)
        stacked = jnp.concatenate([sp16[:, k * sub:(k + 1) * sub] for k in range(nsub)], axis=0)
        after = _dot(stacked, neg_tri(sub))
        ws = [None] * nsub
        for k in reversed(range(nsub)):
            kc = slice(k * sub, (k + 1) * sub)
            w = jnp.exp2(z[:, kc] - sp[:, kc] + after[k * bq:(k + 1) * bq] + c)
            if masked:
                w = jnp.where(allowed[:, kc], w, 0.0)
            ws[k] = w.astype(BF16)
            c = c - jnp.sum(sp[:, kc], axis=1, keepdims=True)
        pv = _dot(jnp.concatenate(ws, axis=1), vb.astype(BF16))
        if first:
            acc_s[:, cols] = pv
        else:
            acc_s[:, cols] += pv
        return c

    cs = tuple(superblock(h, kd_ref[:, h * dh:(h + 1) * dh], vd_ref[:, h * dh:(h + 1) * dh],
                          jnp.zeros((bq, 1), F32), True, True) for h in range(hp))

    n_past = pl.program_id(2) if n_past_static is None else n_past_static

    def body(it, cs):
        rows = pl.ds(pl.multiple_of((n_past - 1 - it) * SB_SUPER, SB_SUPER), SB_SUPER)
        if len(kp_ref.shape) == 2:
            kbs = [kp_ref[rows, h * dh:(h + 1) * dh] for h in range(hp)]
            vbs = [vp_ref[rows, h * dh:(h + 1) * dh] for h in range(hp)]
        else:
            kbs = pltpu.einshape("thd->htd", kp_ref[rows])
            vbs = pltpu.einshape("thd->htd", vp_ref[rows])
        return tuple(superblock(h, kbs[h], vbs[h], cs[h], False, False) for h in range(hp))

    lax.fori_loop(0, n_past, body, cs)
    o_ref[...] = acc_s[...]


def _sb_attention(q, kd, vd, kp, vp, prompt, hp):
    b, tq, width = q.shape
    heads = width // SB_HEAD_DIM
    tp = kp.shape[1]
    assert tp % SB_SUPER == 0 and heads % hp == 0
    if prompt:
        bq, n_past = SB_SUPER, None
    else:
        bq, n_past = tq, tp // SB_SUPER
    assert tq % bq == 0
    gw = hp * SB_HEAD_DIM
    qspec = pl.BlockSpec((None, bq, gw), lambda bi, h, i: (bi, i, h))
    if kp.ndim == 4:
        assert hp == heads
        pspec = pl.BlockSpec((None, tp, heads, SB_HEAD_DIM), lambda bi, h, i: (bi, 0, 0, 0))
    else:
        pspec = pl.BlockSpec((None, tp, gw), lambda bi, h, i: (bi, 0, h))
    return pl.pallas_call(
        functools.partial(_sb_kernel, n_past, hp),
        grid=(b, heads // hp, tq // bq),
        in_specs=[qspec, qspec, qspec, pspec, pspec],
        out_specs=qspec,
        out_shape=jax.ShapeDtypeStruct((b, tq, width), F32),
        scratch_shapes=[pltpu.VMEM((bq, gw), F32), pltpu.VMEM((bq, gw), BF16)],
        compiler_params=_cparams(("parallel", "parallel", "arbitrary"),
                                 4 * tp * gw * 4 + 10 * bq * gw * 4 + 12 * bq * SB_SUPER * 4 * hp + (8 << 20)),
        name="sb_attention",
    )(q, kd, vd, kp, vp)


def _rw_project_kernel(tiles_per_seq, rw_width, x_ref, g_ref, wrw_ref, sh_ref, mu_ref, w0_ref, a0_ref, wa2_ref, g2_ref,
                       r_o, w_o, k_o, v_o, a_o, g_o, last_o, carry):
    @pl.when(pl.program_id(0) % tiles_per_seq == 0)
    def _():
        carry[...] = sh_ref[...]

    p = _dot(_rms(x_ref[...], g_ref[...]).astype(BF16), wrw_ref[...])
    tm = p.shape[0]
    row = lax.broadcasted_iota(jnp.int32, p.shape, 0)
    prev = jnp.where(row == 0, carry[...], pltpu.roll(p, 1, axis=0))
    carry[...] = p[tm - 1:tm, :]
    last_o[...] = p[tm - 1:tm, :]
    ps = p + (prev - p) * mu_ref[...]
    c = rw_width
    r_o[...] = ps[:, 0:c]
    k_o[...] = ps[:, c:2 * c]
    v_o[...] = ps[:, 2 * c:3 * c]
    xwa = ps[:, 3 * c:3 * c + RW_DECAY_RANK + RW_A_RANK]
    lane = lax.broadcasted_iota(jnp.int32, xwa.shape, 1)
    lowrank_in = jnp.where(lane < RW_DECAY_RANK, jnp.tanh(xwa), xwa).astype(BF16)
    wa = _dot(lowrank_in, wa2_ref[...])
    w = -_softplus(-(w0_ref[...] + wa[:, 0:c])) - 0.5
    w_o[...] = jnp.exp(-jnp.exp(w))
    a_o[...] = _sigmoid(a0_ref[...] + wa[:, c:2 * c])
    xg = ps[:, 3 * c + RW_DECAY_RANK + RW_A_RANK:]
    g_o[...] = _dot(_sigmoid(xg).astype(BF16), g2_ref[...])


def _rw_project(x2d, norm_g, w_rw, shift0, mu, w0, a0, wa2, g2, seq_len, tm):
    n, d = x2d.shape
    proj = w_rw.shape[1]
    c = w0.shape[-1]
    assert seq_len % tm == 0 and n % seq_len == 0
    tps = seq_len // tm
    tok = pl.BlockSpec((None, tm, c), lambda i: (i // tps, i % tps, 0))
    per_seq = pl.BlockSpec((None, 1, proj), lambda i: (i // tps, 0, 0))
    return pl.pallas_call(
        functools.partial(_rw_project_kernel, tps, c),
        grid=(n // tm,),
        in_specs=[pl.BlockSpec((tm, d), lambda i: (i, 0)), _resident((1, d)), _resident(w_rw.shape), per_seq,
                  _resident((1, proj)), _resident((1, c)), _resident((1, c)), _resident(wa2.shape), _resident(g2.shape)],
        out_specs=[tok] * 6 + [per_seq],
        out_shape=[jax.ShapeDtypeStruct((n // seq_len, seq_len, c), F32)] * 6
        + [jax.ShapeDtypeStruct((n // seq_len, 1, proj), F32)],
        scratch_shapes=[pltpu.VMEM((1, proj), F32)],
        compiler_params=_cparams(("arbitrary",), w_rw.size * 2 + 6 * tm * proj * 4 + 14 * tm * c * 4 + 4 * tm * d * 4 + (8 << 20)),
        name="rw_project",
    )(x2d, norm_g.reshape(1, d), w_rw, shift0, mu.reshape(1, proj), w0.reshape(1, c), a0.reshape(1, c), wa2, g2)


def _rw_scan_kernel(r_ref, w_ref, k_ref, v_ref, a_ref, kk_p, ka_p, rk_p, lng_p, lnb_p, s0_ref,
                    o_ref, S, kk_s, b_s, kt_s, r_s):
    j = pl.program_id(1)
    nk = S.shape[0]
    tt = r_ref.shape[0]

    @pl.when(j == 0)
    def _():
        S[...] = s0_ref[...]

    def step(t, g):
        r = r_ref[t]
        k = k_ref[t]
        v = v_ref[t]
        a = a_ref[t]
        kk = k * kk_p[...]
        nrm = jnp.sqrt(jnp.sum(kk * kk, axis=0, keepdims=True))
        kk = kk / jnp.maximum(nrm, 1e-12)
        kt = k * (1.0 + (a - 1.0) * ka_p[...])
        g_new = g * w_ref[t]
        inv_g = 1.0 / g_new
        kk_s[...] = kk * g
        b_s[...] = (kk * a) * inv_g
        kt_s[...] = kt * inv_g
        r_s[...] = r * g_new

        def p1(kx, sa):
            return sa + S[kx] * kk_s[pl.ds(kx, 1), :]

        sa = lax.fori_loop(0, nk, p1, jnp.zeros(v.shape, F32), unroll=16)

        def p2(kx, y):
            row = pl.ds(kx, 1)
            s_new = S[kx] - sa * b_s[row, :] + v * kt_s[row, :]
            S[kx] = s_new
            return y + s_new * r_s[row, :]

        y = lax.fori_loop(0, nk, p2, jnp.zeros(v.shape, F32), unroll=16)
        mean = jnp.mean(y, axis=0, keepdims=True)
        d = y - mean
        var = jnp.mean(d * d, axis=0, keepdims=True)
        bonus = jnp.sum(r * kt * rk_p[...], axis=0, keepdims=True) * v
        o_ref[t] = d * lax.rsqrt(var + GN_EPS) * lng_p[...] + lnb_p[...] + bonus
        return g_new

    r_s[...] = lax.fori_loop(0, tt, step, jnp.ones(kk_s.shape, F32))

    def fold(kx, carry):
        S[kx] = S[kx] * r_s[pl.ds(kx, 1), :]
        return carry

    lax.fori_loop(0, nk, fold, 0, unroll=16)


def _rw_scan(rT, wT, kT, vT, aT, params, s0, tt):
    t, n, ctot = rT.shape
    assert t % tt == 0 and ctot % V7X_LANES == 0
    seq = pl.BlockSpec((tt, n, V7X_LANES), lambda c, j: (j, 0, c))
    par = pl.BlockSpec((n, V7X_LANES), lambda c, j: (0, c))
    st = pl.BlockSpec((n, n, V7X_LANES), lambda c, j: (0, 0, c))
    return pl.pallas_call(
        _rw_scan_kernel,
        grid=(ctot // V7X_LANES, t // tt),
        in_specs=[seq] * 5 + [par] * 5 + [st],
        out_specs=[seq, st],
        out_shape=[jax.ShapeDtypeStruct((t, n, ctot), F32), jax.ShapeDtypeStruct((n, n, ctot), F32)],
        scratch_shapes=[pltpu.VMEM((n, V7X_LANES), F32)] * 4,
        compiler_params=_cparams(("parallel", "arbitrary"), 12 * tt * n * V7X_LANES * 4 + 5 * n * n * V7X_LANES * 4 + (8 << 20)),
        name="rw_scan",
    )(rT, wT, kT, vT, aT, *params, s0)


def _mix_kernel(osb_ref, orw_ref, g_ref, gsb_ref, grw_ref, x_ref, wsb_ref, wrw_ref, wout_ref, h_ref):
    sb = _dot(osb_ref[...].astype(BF16), wsb_ref[...])
    rw = _dot((orw_ref[...] * g_ref[...]).astype(BF16), wrw_ref[...])
    mixed = _sigmoid(gsb_ref[...]) * sb + _sigmoid(grw_ref[...]) * rw
    h_ref[...] = x_ref[...] + _dot(mixed.astype(BF16), wout_ref[...])


def _mix(o_sb, o_rw, g, gate_sb, gate_rw, x2d, wsb, wrw, wout, tm=256):
    n, d = x2d.shape
    c = o_sb.shape[1]
    assert n % tm == 0
    half = pl.BlockSpec((tm, c), lambda i: (i, 0))
    full = pl.BlockSpec((tm, d), lambda i: (i, 0))
    return pl.pallas_call(
        _mix_kernel,
        grid=(n // tm,),
        in_specs=[half, half, half, full, full, full, _resident(wsb.shape), _resident(wrw.shape), _resident(wout.shape)],
        out_specs=full,
        out_shape=jax.ShapeDtypeStruct((n, d), F32),
        compiler_params=_cparams(("parallel",), (wsb.size + wrw.size + wout.size) * 2 + 2 * tm * (3 * c + 4 * d) * 4 + (12 << 20)),
        name="branch_mix",
    )(o_sb, o_rw, g, gate_sb, gate_rw, x2d, wsb, wrw, wout)


PEER_RANKS = PEER_TOPK + 1
PEER_RANK_ROWS = 24
PEER_UNRANKED = 127.0


def _top_ranks(s, t_ref, want_ranks):
    t_ref[...] = jnp.full(t_ref.shape, NEG_BIG, F32)
    ranks = jnp.full(s.shape, PEER_UNRANKED, F32) if want_ranks else None
    for rnk in range(PEER_RANKS):
        m = jnp.max(s, axis=0, keepdims=True)
        t_ref[rnk:rnk + 1, :] = m
        hit = s == m
        if want_ranks:
            ranks = jnp.where(hit, float(rnk), ranks)
        s = jnp.where(hit, NEG_BIG, s)
    return ranks


def _peer_select_kernel(h_ref, g_ref, wq_ref, k1_ref, k2_ref, xn_o, lim_o, coef_o, rank2_o, e2_o, t1_s, t2_s):
    xn = _rms(h_ref[...], g_ref[...]).astype(BF16)
    xn_o[...] = xn
    q = _dot(xn, wq_ref[...])
    k1 = k1_ref[...].astype(BF16)
    k2 = k2_ref[...].astype(BF16)
    sub = V7X_SUBLANES

    for h in range(PEER_HEADS):
        base = 2 * PEER_HALF * h
        s1 = _dot_nt(k1, q[:, base:base + PEER_HALF].astype(BF16))
        s2 = _dot_nt(k2, q[:, base + PEER_HALF:base + 2 * PEER_HALF].astype(BF16))
        _top_ranks(s1, t1_s, False)
        rank2 = _top_ranks(s2, t2_s, True)
        t1 = t1_s[...]
        t2 = t2_s[...]
        cand = jnp.concatenate(
            [t1[0:1] + t2[0:sub], t1[0:1] + t2[sub:2 * sub], t1[0:1] + t2[2 * sub:3 * sub],
             t2[0:1] + t1[0:sub], t2[0:1] + t1[sub:2 * sub], t2[0:1] + t1[2 * sub:3 * sub],
             t1[1:2] + t2[0:sub], t2[1:2] + t1[0:sub],
             t1[2:3] + t2[0:sub], t1[3:4] + t2[0:sub], t1[4:5] + t2[0:sub]], axis=0)
        top = jnp.max(cand, axis=0, keepdims=True)
        zsum = jnp.zeros(top.shape, F32)
        lo16 = top
        for rnk in range(PEER_RANKS):
            m = jnp.max(cand, axis=0, keepdims=True)
            if rnk < PEER_TOPK:
                zsum = zsum + jnp.exp(m - top)
                lo16 = m
            else:
                tau = 0.5 * (lo16 + m)
            cand = jnp.where(cand == m, NEG_BIG, cand)
        lim = jnp.zeros(s1.shape, F32)
        for a in range(PEER_TOPK):
            width = jnp.sum(jnp.where(t2 > tau - t1[a:a + 1], 1.0, 0.0), axis=0, keepdims=True)
            lim = jnp.where(s1 == t1[a:a + 1], width, lim)
        lim_o[h] = lim
        coef_o[h] = jnp.exp(s1 - t1[0:1]) / zsum
        rank2_o[h] = rank2.astype(BF16)
        e2_o[h] = jnp.exp(s2 - t2[0:1]).astype(BF16)


def _peer_select(h2d, g, wq, k1, k2, tn):
    n, d = h2d.shape
    assert n % tn == 0
    sel = pl.BlockSpec((PEER_HEADS, PEER_NKEYS, tn), lambda i: (0, 0, i))
    sel_f32 = jax.ShapeDtypeStruct((PEER_HEADS, PEER_NKEYS, n), F32)
    sel_bf16 = jax.ShapeDtypeStruct((PEER_HEADS, PEER_NKEYS, n), BF16)
    return pl.pallas_call(
        _peer_select_kernel,
        grid=(n // tn,),
        in_specs=[pl.BlockSpec((tn, d), lambda i: (i, 0)), _resident((1, d)), _resident(wq.shape),
                  _resident(k1.shape), _resident(k2.shape)],
        out_specs=[pl.BlockSpec((tn, d), lambda i: (i, 0)), sel, sel, sel, sel],
        out_shape=[jax.ShapeDtypeStruct((n, d), BF16), sel_f32, sel_f32, sel_bf16, sel_bf16],
        scratch_shapes=[pltpu.VMEM((PEER_RANK_ROWS, tn), F32)] * 2,
        compiler_params=_cparams(("parallel",), wq.size * 2 + 8 * PEER_HEADS * PEER_NKEYS * tn * 4 + 8 * tn * d * 4 + (12 << 20)),
        name="peer_select",
    )(h2d, g.reshape(1, d), wq, k1, k2)


def _peer_main_kernel(xn_ref, u_ref, v_ref, lim_ref, coef_ref, rank2_ref, e2_ref, h_ref, g_ref, y_ref, acc):
    j = pl.program_id(1)
    eb = u_ref.shape[0]
    nib = eb // PEER_NKEYS
    tile = (PEER_NKEYS, xn_ref.shape[0])

    @pl.when(j == 0)
    def _():
        acc[...] = jnp.zeros(acc.shape, F32)

    hid = _dot_nt(u_ref[...], xn_ref[...])
    parts = []
    for ii in range(nib):
        row = pl.ds(j * nib + ii, 1)
        wgt = jnp.zeros(tile, BF16)
        for h in range(PEER_HEADS):
            lim = jnp.broadcast_to(lim_ref[h, row, :], tile).astype(BF16)
            coef = jnp.broadcast_to(coef_ref[h, row, :], tile).astype(BF16)
            picked = jnp.where(rank2_ref[h] < lim, e2_ref[h], jnp.zeros((), BF16))
            wgt = wgt + coef * picked
        x = hid[ii * PEER_NKEYS:(ii + 1) * PEER_NKEYS, :].astype(BF16)
        gelu = (0.5 * x) * (1.0 + lax.erf(x * (2.0 ** -0.5)))
        parts.append(gelu * wgt)
    gated = jnp.concatenate(parts, axis=0)
    acc[...] += _dot_tn(gated, v_ref[...])

    @pl.when(j == pl.num_programs(1) - 1)
    def _():
        y_ref[...] = _rms(h_ref[...] + acc[...], g_ref[...])


def _peer_main(xn, u, v, lim, coef, rank2, e2, h2d, g, tn, eb):
    n, d = h2d.shape
    e = u.shape[0]
    assert n % tn == 0 and e % eb == 0 and eb % PEER_NKEYS == 0
    sel = pl.BlockSpec((PEER_HEADS, PEER_NKEYS, tn), lambda i, j: (0, 0, i))
    tok = pl.BlockSpec((tn, d), lambda i, j: (i, 0))
    tab = pl.BlockSpec((eb, d), lambda i, j: (j, 0))
    vmem = 2 * tn * d * 2 + 4 * eb * d * 2 + 6 * PEER_HEADS * PEER_NKEYS * tn * 4 + 5 * tn * d * 4 + (8 << 20)
    return pl.pallas_call(
        _peer_main_kernel,
        grid=(n // tn, e // eb),
        in_specs=[tok, tab, tab, sel, sel, sel, sel, tok, _resident((1, d))],
        out_specs=tok,
        out_shape=jax.ShapeDtypeStruct((n, d), F32),
        scratch_shapes=[pltpu.VMEM((tn, d), F32)],
        compiler_params=_cparams(("parallel", "arbitrary"), vmem),
        name="peer_main",
    )(xn, u, v, lim, coef, rank2, e2, h2d, g.reshape(1, d))


def _to_chain_major(x, heads):
    b, t, c = x.shape
    n = c // heads
    return x.reshape(b, t, heads, n).transpose(1, 3, 0, 2).reshape(t, n, b * heads)


def _from_chain_major(x, b):
    t, n, ch = x.shape
    heads = ch // b
    return x.reshape(t, n, b, heads).transpose(2, 0, 3, 1).reshape(b, t, heads * n)


def _head_param(p, heads, b):
    n = p.size // heads
    return jnp.tile(p.reshape(heads, n).T, (1, b))


def _layer(x, past_k, past_v, wkv0, shift0, wts, sb_hp, rw_tt, peer_tn, peer_eb):
    b, t, d = x.shape
    n_tok = b * t
    x2d = x.reshape(n_tok, d)
    sbw = wts["wq"].shape[1]
    rww = wts["w0"].shape[-1]
    rw_heads = rww // RW_HEAD_DIM
    proj = wts["w_rw"].shape[1]

    q, k, v, gate_sb, gate_rw = _norm_matmul(
        x2d, wts["norm1_g"], [wts["wq"], wts["wk"], wts["wv"], wts["w_gsb"], wts["w_grw"]])

    q3, k3, v3 = (a.reshape(b, t, sbw) for a in (q, k, v))
    heads_sb = sbw // SB_HEAD_DIM
    if past_k is None:
        o_sb = _sb_attention(q3, k3, v3, k3, v3, prompt=True, hp=sb_hp)
        wkv0 = jnp.zeros((b, rw_heads, RW_HEAD_DIM, RW_HEAD_DIM), F32)
        shift0 = jnp.zeros((b, 1, proj), F32)
    else:
        o_sb = _sb_attention(q3, k3, v3, past_k, past_v, prompt=False, hp=sb_hp)

    r, w, kr, vr, a, g, shift_new = _rw_project(x2d, wts["norm1_g"], wts["w_rw"], shift0, wts["rw_mu"], wts["w0"],
                                                wts["a0"], wts["wa2"], wts["g2"], seq_len=t, tm=min(t, 256))
    seqs = [_to_chain_major(s, rw_heads) for s in (r, w, kr, vr, a)]
    params = [_head_param(wts[nm], rw_heads, b) for nm in ("k_k", "k_a", "r_k", "ln_g", "ln_b")]
    s0 = wkv0.astype(F32).transpose(3, 2, 0, 1).reshape(RW_HEAD_DIM, RW_HEAD_DIM, b * rw_heads)
    oT, sT = _rw_scan(*seqs, params, s0, tt=rw_tt)
    o_rw = _from_chain_major(oT, b).reshape(n_tok, rww)
    wkv_new = sT.reshape(RW_HEAD_DIM, RW_HEAD_DIM, b, rw_heads).transpose(2, 3, 1, 0)

    h = _mix(o_sb.reshape(n_tok, sbw), o_rw, g.reshape(n_tok, rww), gate_sb, gate_rw, x2d,
             wts["w_bsb"], wts["w_brw"], wts["w_out"])
    xn2, lim, coef, rank2, e2 = _peer_select(h, wts["norm2_g"], wts["peer_wq"], wts["peer_k1"], wts["peer_k2"], tn=256)
    y = _peer_main(xn2, wts["peer_u"], wts["peer_v"], lim, coef, rank2, e2, h, wts["final_g"], tn=peer_tn, eb=peer_eb)
    kv_shape = (b, t, heads_sb, SB_HEAD_DIM)
    return y.reshape(b, t, d), k.reshape(kv_shape), v.reshape(kv_shape), wkv_new, shift_new


def kernel(x_prompt, x_sample, cache_sb_k, cache_sb_v, state_rwkv_wkv, state_rwkv_shift, norm1_g, w_in, rw_mu, rw_w0, rw_w2, rw_a0, rw_a2, rw_g2, rw_k_k, rw_k_a, rw_r_k, rw_ln_g, rw_ln_b, w_branch_sb, w_branch_rw, w_out, norm2_g, peer_wq, peer_k1, peer_k2, peer_u, peer_v, final_norm_g):
    assert norm1_g.shape[0] == 1, "single-layer stack"
    d = x_prompt.shape[-1]
    sbw = w_branch_sb.shape[1]
    rww = w_branch_rw.shape[1]
    proj = rw_mu.shape[1]
    w = w_in[0]
    o = 0
    cols = {}
    for nm, width in (("wq", sbw), ("wk", sbw), ("wv", sbw), ("w_rw", proj), ("w_gsb", d), ("w_grw", d)):
        cols[nm] = w[:, o:o + width].astype(BF16)
        o += width
    zeros = jnp.zeros((RW_DECAY_RANK, rww), F32)
    wa2 = jnp.concatenate([jnp.concatenate([rw_w2[0], zeros], axis=1),
                           jnp.concatenate([zeros, rw_a2[0]], axis=1)], axis=0).astype(BF16)
    wts = dict(cols, norm1_g=norm1_g[0], rw_mu=rw_mu[0], w0=rw_w0[0], a0=rw_a0[0], wa2=wa2, g2=rw_g2[0].astype(BF16),
               k_k=rw_k_k[0], k_a=rw_k_a[0], r_k=rw_r_k[0].reshape(-1), ln_g=rw_ln_g[0], ln_b=rw_ln_b[0],
               w_bsb=w_branch_sb[0].astype(BF16), w_brw=w_branch_rw[0].astype(BF16), w_out=w_out[0].astype(BF16),
               norm2_g=norm2_g[0], peer_wq=peer_wq[0].astype(BF16), peer_k1=peer_k1[0], peer_k2=peer_k2[0],
               peer_u=peer_u[0].astype(BF16), peer_v=peer_v[0].astype(BF16), final_g=final_norm_g)

    sb_heads = sbw // SB_HEAD_DIM
    yp, kp, vp, wp, sp = _layer(x_prompt, None, None, None, None, wts, sb_hp=4, rw_tt=32, peer_tn=512, peer_eb=1024)
    ys, ks, vs, ws, ss = _layer(x_sample, cache_sb_k[0], cache_sb_v[0], state_rwkv_wkv[0], state_rwkv_shift[0], wts,
                                sb_hp=sb_heads, rw_tt=32, peer_tn=512, peer_eb=1024)
    st = lambda t: t[None]
    return (yp, ys, st(kp), st(vp), st(wp.astype(x_prompt.dtype)), st(sp), st(ks), st(vs), st(ws.astype(x_sample.dtype)), st(ss))
```

```python
import functools

import jax
import jax.numpy as jnp
from jax import lax
from jax.experimental import pallas as pl
from jax.experimental.pallas import tpu as pltpu

F32 = jnp.float32
BF16 = jnp.bfloat16

RMS_EPS = 1e-6
GN_EPS = 64e-5
SB_HEAD_DIM = 128
RW_HEAD_DIM = 64
RW_DECAY_RANK = 64
RW_A_RANK = 64
RW_GATE_RANK = 128
PEER_HEADS = 8
PEER_NKEYS = 128
PEER_HALF = 128
PEER_TOPK = 16

V7X_LANES = 128
V7X_SUBLANES = 8
V7X_VMEM_LIMIT_BYTES = 60000 * 1024

NEG_BIG = -1e30


def _cparams(semantics, vmem_bytes):
    return pltpu.CompilerParams(dimension_semantics=semantics,
                                vmem_limit_bytes=int(min(V7X_VMEM_LIMIT_BYTES, vmem_bytes)))


def _sigmoid(x):
    return 1.0 / (1.0 + jnp.exp(-x))


def _softplus(x):
    return jnp.maximum(x, 0.0) + jnp.log1p(jnp.exp(-jnp.abs(x)))


def _rms(x, g):
    return x * lax.rsqrt(jnp.mean(x * x, axis=-1, keepdims=True) + RMS_EPS) * g


def _dot(a, b):
    return jnp.dot(a, b, preferred_element_type=F32)


def _dot_nt(a, b):
    return lax.dot_general(a, b, (((1,), (1,)), ((), ())), preferred_element_type=F32)


def _dot_tn(a, b):
    return lax.dot_general(a, b, (((0,), (0,)), ((), ())), preferred_element_type=F32)


def _resident(shape):
    nd = len(shape)
    return pl.BlockSpec(shape, lambda *_: (0,) * nd, pipeline_mode=pl.Buffered(1))


def _norm_matmul_kernel(n_w, x_ref, g_ref, *refs):
    xn = _rms(x_ref[...], g_ref[...]).astype(BF16)
    for w_ref, o_ref in zip(refs[:n_w], refs[n_w:]):
        o_ref[...] = _dot(xn, w_ref[...])


def _norm_matmul(x2d, g, ws, tm=256):
    n, d = x2d.shape
    assert n % tm == 0
    wbytes = sum(w.size * 2 for w in ws)
    obytes = sum(2 * tm * w.shape[1] * 4 for w in ws)
    return pl.pallas_call(
        functools.partial(_norm_matmul_kernel, len(ws)),
        grid=(n // tm,),
        in_specs=[pl.BlockSpec((tm, d), lambda i: (i, 0)), _resident((1, d))] + [_resident(w.shape) for w in ws],
        out_specs=[pl.BlockSpec((tm, w.shape[1]), lambda i: (i, 0)) for w in ws],
        out_shape=[jax.ShapeDtypeStruct((n, w.shape[1]), F32) for w in ws],
        compiler_params=_cparams(("parallel",), wbytes + 2 * obytes + 6 * tm * d * 4 + (8 << 20)),
        name="norm_matmul",
    )(x2d, g.reshape(1, d), *ws)


SB_SUPER = 512
SB_SUB = 256
LOG2E = 1.4426950408889634
SB_DEAD_LOG2 = 160.0


def _sb_kernel(n_past_static, hp, q_ref, kd_ref, vd_ref, kp_ref, vp_ref, o_ref, acc_s, q_s):
    bq = q_ref.shape[0]
    dh = SB_HEAD_DIM
    q_s[...] = (q_ref[...] * (dh ** -0.5 * LOG2E)).astype(BF16)

    def neg_tri(n):
        r = lax.broadcasted_iota(jnp.int32, (n, n), 0)
        c = lax.broadcasted_iota(jnp.int32, (n, n), 1)
        return jnp.where(r > c, -1.0, 0.0).astype(BF16)

    def superblock(h, kb, vb, c, masked, first):
        cols = slice(h * dh, (h + 1) * dh)
        ks = kb.shape[0]
        sub = min(SB_SUB, ks)
        nsub = ks // sub
        z = _dot_nt(q_s[:, cols], kb.astype(BF16))
        sp = jnp.maximum(z, 0.0) + jnp.log2(1.0 + jnp.exp2(-jnp.abs(z)))
        if masked:
            allowed = lax.broadcasted_iota(jnp.int32, (bq, ks), 1) < lax.broadcasted_iota(jnp.int32, (bq, ks), 0)
            sp = jnp.where(allowed, sp, 0.0)
        sp16 = sp.astype(BF16)
        stacked = jnp.concatenate([sp16[:, k * sub:(k + 1) * sub] for k in range(nsub)], axis=0)
        after = _dot(stacked, neg_tri(sub))
        ws = [None] * nsub
        for k in reversed(range(nsub)):
            kc = slice(k * sub, (k + 1) * sub)
            w = jnp.exp2(z[:, kc] - sp[:, kc] + after[k * bq:(k + 1) * bq] + c)
            if masked:
                w = jnp.where(allowed[:, kc], w, 0.0)
            ws[k] = w.astype(BF16)
            c = c - jnp.sum(sp[:, kc], axis=1, keepdims=True)
        pv = _dot(jnp.concatenate(ws, axis=1), vb.astype(BF16))
        if first:
            acc_s[:, cols] = pv
        else:
            acc_s[:, cols] += pv
        return c

    cs = tuple(superblock(h, kd_ref[:, h * dh:(h + 1) * dh], vd_ref[:, h * dh:(h + 1) * dh],
                          jnp.zeros((bq, 1), F32), True, True) for h in range(hp))

    n_past = pl.program_id(2) if n_past_static is None else n_past_static

    def body(it, cs):
        rows = pl.ds(pl.multiple_of((n_past - 1 - it) * SB_SUPER, SB_SUPER), SB_SUPER)
        if len(kp_ref.shape) == 2:
            kbs = [kp_ref[rows, h * dh:(h + 1) * dh] for h in range(hp)]
            vbs = [vp_ref[rows, h * dh:(h + 1) * dh] for h in range(hp)]
        else:
            kbs = pltpu.einshape("thd->htd", kp_ref[rows])
            vbs = pltpu.einshape("thd->htd", vp_ref[rows])
        return tuple(superblock(h, kbs[h], vbs[h], cs[h], False, False) for h in range(hp))

    def live(cs):
        top = cs[0]
        for c in cs[1:]:
            top = jnp.maximum(top, c)
        return jnp.max(top) > -SB_DEAD_LOG2

    def w_cond(state):
        it, alive, _ = state
        return jnp.logical_and(it < n_past, alive)

    def w_body(state):
        it, _, cs = state
        cs = body(it, cs)
        return it + 1, live(cs), cs

    lax.while_loop(w_cond, w_body, (jnp.int32(0), live(cs), cs))
    o_ref[...] = acc_s[...]


def _sb_attention(q, kd, vd, kp, vp, prompt, hp):
    b, tq, width = q.shape
    heads = width // SB_HEAD_DIM
    tp = kp.shape[1]
    assert tp % SB_SUPER == 0 and heads % hp == 0
    if prompt:
        bq, n_past = SB_SUPER, None
    else:
        bq, n_past = tq, tp // SB_SUPER
    assert tq % bq == 0
    gw = hp * SB_HEAD_DIM
    qspec = pl.BlockSpec((None, bq, gw), lambda bi, h, i: (bi, i, h))
    if kp.ndim == 4:
        assert hp == heads
        pspec = pl.BlockSpec((None, tp, heads, SB_HEAD_DIM), lambda bi, h, i: (bi, 0, 0, 0))
    else:
        pspec = pl.BlockSpec((None, tp, gw), lambda bi, h, i: (bi, 0, h))
    return pl.pallas_call(
        functools.partial(_sb_kernel, n_past, hp),
        grid=(b, heads // hp, tq // bq),
        in_specs=[qspec, qspec, qspec, pspec, pspec],
        out_specs=qspec,
        out_shape=jax.ShapeDtypeStruct((b, tq, width), F32),
        scratch_shapes=[pltpu.VMEM((bq, gw), F32), pltpu.VMEM((bq, gw), BF16)],
        compiler_params=_cparams(("parallel", "parallel", "arbitrary"),
                                 4 * tp * gw * 4 + 10 * bq * gw * 4 + 12 * bq * SB_SUPER * 4 * hp + (8 << 20)),
        name="sb_attention",
    )(q, kd, vd, kp, vp)


def _rw_project_kernel(tiles_per_seq, rw_width, x_ref, g_ref, wrw_ref, sh_ref, mu_ref, w0_ref, a0_ref, wa2_ref, g2_ref,
                       r_o, w_o, k_o, v_o, a_o, g_o, last_o, carry):
    @pl.when(pl.program_id(0) % tiles_per_seq == 0)
    def _():
        carry[...] = sh_ref[...]

    p = _dot(_rms(x_ref[...], g_ref[...]).astype(BF16), wrw_ref[...])
    tm = p.shape[0]
    row = lax.broadcasted_iota(jnp.int32, p.shape, 0)
    prev = jnp.where(row == 0, carry[...], pltpu.roll(p, 1, axis=0))
    carry[...] = p[tm - 1:tm, :]
    last_o[...] = p[tm - 1:tm, :]
    ps = p + (prev - p) * mu_ref[...]
    c = rw_width
    r_o[...] = ps[:, 0:c]
    k_o[...] = ps[:, c:2 * c]
    v_o[...] = ps[:, 2 * c:3 * c]
    xwa = ps[:, 3 * c:3 * c + RW_DECAY_RANK + RW_A_RANK]
    lane = lax.broadcasted_iota(jnp.int32, xwa.shape, 1)
    lowrank_in = jnp.where(lane < RW_DECAY_RANK, jnp.tanh(xwa), xwa).astype(BF16)
    wa = _dot(lowrank_in, wa2_ref[...])
    w = -_softplus(-(w0_ref[...] + wa[:, 0:c])) - 0.5
    w_o[...] = jnp.exp(-jnp.exp(w))
    a_o[...] = _sigmoid(a0_ref[...] + wa[:, c:2 * c])
    xg = ps[:, 3 * c + RW_DECAY_RANK + RW_A_RANK:]
    g_o[...] = _dot(_sigmoid(xg).astype(BF16), g2_ref[...])


def _rw_project(x2d, norm_g, w_rw, shift0, mu, w0, a0, wa2, g2, seq_len, tm):
    n, d = x2d.shape
    proj = w_rw.shape[1]
    c = w0.shape[-1]
    assert seq_len % tm == 0 and n % seq_len == 0
    tps = seq_len // tm
    tok = pl.BlockSpec((None, tm, c), lambda i: (i // tps, i % tps, 0))
    per_seq = pl.BlockSpec((None, 1, proj), lambda i: (i // tps, 0, 0))
    return pl.pallas_call(
        functools.partial(_rw_project_kernel, tps, c),
        grid=(n // tm,),
        in_specs=[pl.BlockSpec((tm, d), lambda i: (i, 0)), _resident((1, d)), _resident(w_rw.shape), per_seq,
                  _resident((1, proj)), _resident((1, c)), _resident((1, c)), _resident(wa2.shape), _resident(g2.shape)],
        out_specs=[tok] * 6 + [per_seq],
        out_shape=[jax.ShapeDtypeStruct((n // seq_len, seq_len, c), F32)] * 6
        + [jax.ShapeDtypeStruct((n // seq_len, 1, proj), F32)],
        scratch_shapes=[pltpu.VMEM((1, proj), F32)],
        compiler_params=_cparams(("arbitrary",), w_rw.size * 2 + 6 * tm * proj * 4 + 14 * tm * c * 4 + 4 * tm * d * 4 + (8 << 20)),
        name="rw_project",
    )(x2d, norm_g.reshape(1, d), w_rw, shift0, mu.reshape(1, proj), w0.reshape(1, c), a0.reshape(1, c), wa2, g2)


def _rw_scan_kernel(r_ref, w_ref, k_ref, v_ref, a_ref, kk_p, ka_p, rk_p, lng_p, lnb_p, s0_ref,
                    o_ref, S, kk_s, b_s, kt_s, r_s, bonus_s, g_s):
    j = pl.program_id(1)
    nk = S.shape[0]
    tt = r_ref.shape[0]

    @pl.when(j == 0)
    def _():
        S[...] = s0_ref[...]

    def prep(t, g):
        r = r_ref[t]
        k = k_ref[t]
        a = a_ref[t]
        kk = k * kk_p[...]
        nrm = jnp.sqrt(jnp.sum(kk * kk, axis=0, keepdims=True))
        kk = kk / jnp.maximum(nrm, 1e-12)
        kt = k * (1.0 + (a - 1.0) * ka_p[...])
        g_new = g * w_ref[t]
        inv_g = 1.0 / g_new
        kk_s[t] = kk * g
        b_s[t] = (kk * a) * inv_g
        kt_s[t] = kt * inv_g
        r_s[t] = r * g_new
        bonus_s[t] = jnp.sum(r * kt * rk_p[...], axis=0, keepdims=True) * v_ref[t]
        return g_new

    g_s[...] = lax.fori_loop(0, tt, prep, jnp.ones(g_s.shape, F32), unroll=2)

    def step(t, carry):
        v = v_ref[t]

        def p1(kx, sa):
            return sa + S[kx] * kk_s[t, pl.ds(kx, 1), :]

        sa = lax.fori_loop(0, nk, p1, jnp.zeros(v.shape, F32), unroll=16)

        def p2(kx, y):
            row = pl.ds(kx, 1)
            s_new = S[kx] - sa * b_s[t, row, :] + v * kt_s[t, row, :]
            S[kx] = s_new
            return y + s_new * r_s[t, row, :]

        o_ref[t] = lax.fori_loop(0, nk, p2, jnp.zeros(v.shape, F32), unroll=16)
        return carry

    lax.fori_loop(0, tt, step, 0)

    def post(t, carry):
        y = o_ref[t]
        mean = jnp.mean(y, axis=0, keepdims=True)
        d = y - mean
        var = jnp.mean(d * d, axis=0, keepdims=True)
        o_ref[t] = d * lax.rsqrt(var + GN_EPS) * lng_p[...] + lnb_p[...] + bonus_s[t]
        return carry

    lax.fori_loop(0, tt, post, 0, unroll=2)

    def fold(kx, carry):
        S[kx] = S[kx] * g_s[pl.ds(kx, 1), :]
        return carry

    lax.fori_loop(0, nk, fold, 0, unroll=16)


def _rw_scan(rT, wT, kT, vT, aT, params, s0, tt):
    t, n, ctot = rT.shape
    assert t % tt == 0 and ctot % V7X_LANES == 0
    seq = pl.BlockSpec((tt, n, V7X_LANES), lambda c, j: (j, 0, c))
    par = pl.BlockSpec((n, V7X_LANES), lambda c, j: (0, c))
    st = pl.BlockSpec((n, n, V7X_LANES), lambda c, j: (0, 0, c))
    return pl.pallas_call(
        _rw_scan_kernel,
        grid=(ctot // V7X_LANES, t // tt),
        in_specs=[seq] * 5 + [par] * 5 + [st],
        out_specs=[seq, st],
        out_shape=[jax.ShapeDtypeStruct((t, n, ctot), F32), jax.ShapeDtypeStruct((n, n, ctot), F32)],
        scratch_shapes=[pltpu.VMEM((tt, n, V7X_LANES), F32)] * 5 + [pltpu.VMEM((n, V7X_LANES), F32)],
        compiler_params=_cparams(("parallel", "arbitrary"), 17 * tt * n * V7X_LANES * 4 + 5 * n * n * V7X_LANES * 4 + (8 << 20)),
        name="rw_scan",
    )(rT, wT, kT, vT, aT, *params, s0)


def _mix_kernel(osb_ref, orw_ref, g_ref, gsb_ref, grw_ref, x_ref, wsb_ref, wrw_ref, wout_ref, h_ref):
    sb = _dot(osb_ref[...].astype(BF16), wsb_ref[...])
    rw = _dot((orw_ref[...] * g_ref[...]).astype(BF16), wrw_ref[...])
    mixed = _sigmoid(gsb_ref[...]) * sb + _sigmoid(grw_ref[...]) * rw
    h_ref[...] = x_ref[...] + _dot(mixed.astype(BF16), wout_ref[...])


def _mix(o_sb, o_rw, g, gate_sb, gate_rw, x2d, wsb, wrw, wout, tm=256):
    n, d = x2d.shape
    c = o_sb.shape[1]
    assert n % tm == 0
    half = pl.BlockSpec((tm, c), lambda i: (i, 0))
    full = pl.BlockSpec((tm, d), lambda i: (i, 0))
    return pl.pallas_call(
        _mix_kernel,
        grid=(n // tm,),
        in_specs=[half, half, half, full, full, full, _resident(wsb.shape), _resident(wrw.shape), _resident(wout.shape)],
        out_specs=full,
        out_shape=jax.ShapeDtypeStruct((n, d), F32),
        compiler_params=_cparams(("parallel",), (wsb.size + wrw.size + wout.size) * 2 + 2 * tm * (3 * c + 4 * d) * 4 + (12 << 20)),
        name="branch_mix",
    )(o_sb, o_rw, g, gate_sb, gate_rw, x2d, wsb, wrw, wout)


PEER_RANKS = PEER_TOPK + 1
PEER_RANK_ROWS = 24
PEER_UNRANKED = 127.0


def _top_ranks(s, t_ref, want_ranks):
    t_ref[...] = jnp.full(t_ref.shape, NEG_BIG, F32)
    ranks = jnp.full(s.shape, PEER_UNRANKED, F32) if want_ranks else None
    for rnk in range(PEER_RANKS):
        m = jnp.max(s, axis=0, keepdims=True)
        t_ref[rnk:rnk + 1, :] = m
        hit = s == m
        if want_ranks:
            ranks = jnp.where(hit, float(rnk), ranks)
        s = jnp.where(hit, NEG_BIG, s)
    return ranks


def _peer_select_kernel(h_ref, g_ref, wq_ref, k1_ref, k2_ref, xn_o, lim_o, coef_o, rank2_o, e2_o, t1_s, t2_s):
    xn = _rms(h_ref[...], g_ref[...]).astype(BF16)
    xn_o[...] = xn
    q = _dot(xn, wq_ref[...])
    k1 = k1_ref[...].astype(BF16)
    k2 = k2_ref[...].astype(BF16)
    sub = V7X_SUBLANES

    for h in range(PEER_HEADS):
        base = 2 * PEER_HALF * h
        s1 = _dot_nt(k1, q[:, base:base + PEER_HALF].astype(BF16))
        s2 = _dot_nt(k2, q[:, base + PEER_HALF:base + 2 * PEER_HALF].astype(BF16))
        _top_ranks(s1, t1_s, False)
        rank2 = _top_ranks(s2, t2_s, True)
        t1 = t1_s[...]
        t2 = t2_s[...]
        cand = jnp.concatenate(
            [t1[0:1] + t2[0:sub], t1[0:1] + t2[sub:2 * sub], t1[0:1] + t2[2 * sub:3 * sub],
             t2[0:1] + t1[0:sub], t2[0:1] + t1[sub:2 * sub], t2[0:1] + t1[2 * sub:3 * sub],
             t1[1:2] + t2[0:sub], t2[1:2] + t1[0:sub],
             t1[2:3] + t2[0:sub], t1[3:4] + t2[0:sub], t1[4:5] + t2[0:sub]], axis=0)
        top = jnp.max(cand, axis=0, keepdims=True)
        zsum = jnp.zeros(top.shape, F32)
        lo16 = top
        for rnk in range(PEER_RANKS):
            m = jnp.max(cand, axis=0, keepdims=True)
            if rnk < PEER_TOPK:
                zsum = zsum + jnp.exp(m - top)
                lo16 = m
            else:
                tau = 0.5 * (lo16 + m)
            cand = jnp.where(cand == m, NEG_BIG, cand)
        lim = jnp.zeros(s1.shape, F32)
        for a in range(PEER_TOPK):
            width = jnp.sum(jnp.where(t2 > tau - t1[a:a + 1], 1.0, 0.0), axis=0, keepdims=True)
            lim = jnp.where(s1 == t1[a:a + 1], width, lim)
        lim_o[h] = lim
        coef_o[h] = jnp.exp(s1 - t1[0:1]) / zsum
        rank2_o[h] = rank2.astype(BF16)
        e2_o[h] = jnp.exp(s2 - t2[0:1]).astype(BF16)


def _peer_select(h2d, g, wq, k1, k2, tn):
    n, d = h2d.shape
    assert n % tn == 0
    sel = pl.BlockSpec((PEER_HEADS, PEER_NKEYS, tn), lambda i: (0, 0, i))
    sel_f32 = jax.ShapeDtypeStruct((PEER_HEADS, PEER_NKEYS, n), F32)
    sel_bf16 = jax.ShapeDtypeStruct((PEER_HEADS, PEER_NKEYS, n), BF16)
    return pl.pallas_call(
        _peer_select_kernel,
        grid=(n // tn,),
        in_specs=[pl.BlockSpec((tn, d), lambda i: (i, 0)), _resident((1, d)), _resident(wq.shape),
                  _resident(k1.shape), _resident(k2.shape)],
        out_specs=[pl.BlockSpec((tn, d), lambda i: (i, 0)), sel, sel, sel, sel],
        out_shape=[jax.ShapeDtypeStruct((n, d), BF16), sel_f32, sel_f32, sel_bf16, sel_bf16],
        scratch_shapes=[pltpu.VMEM((PEER_RANK_ROWS, tn), F32)] * 2,
        compiler_params=_cparams(("parallel",), wq.size * 2 + 8 * PEER_HEADS * PEER_NKEYS * tn * 4 + 8 * tn * d * 4 + (12 << 20)),
        name="peer_select",
    )(h2d, g.reshape(1, d), wq, k1, k2)


def _peer_main_kernel(xn_ref, u_ref, v_ref, lim_ref, coef_ref, rank2_ref, e2_ref, h_ref, g_ref, y_ref, acc):
    j = pl.program_id(1)
    eb = u_ref.shape[0]
    nib = eb // PEER_NKEYS
    tile = (PEER_NKEYS, xn_ref.shape[0])

    @pl.when(j == 0)
    def _():
        acc[...] = jnp.zeros(acc.shape, F32)

    hid = _dot_nt(u_ref[...], xn_ref[...])
    parts = []
    for ii in range(nib):
        row = pl.ds(j * nib + ii, 1)
        wgt = jnp.zeros(tile, BF16)
        for h in range(PEER_HEADS):
            lim = jnp.broadcast_to(lim_ref[h, row, :], tile).astype(BF16)
            coef = jnp.broadcast_to(coef_ref[h, row, :], tile).astype(BF16)
            picked = jnp.where(rank2_ref[h] < lim, e2_ref[h], jnp.zeros((), BF16))
            wgt = wgt + coef * picked
        x = hid[ii * PEER_NKEYS:(ii + 1) * PEER_NKEYS, :].astype(BF16)
        gelu = (0.5 * x) * (1.0 + lax.erf(x * (2.0 ** -0.5)))
        parts.append(gelu * wgt)
    gated = jnp.concatenate(parts, axis=0)
    acc[...] += _dot_tn(gated, v_ref[...])

    @pl.when(j == pl.num_programs(1) - 1)
    def _():
        y_ref[...] = _rms(h_ref[...] + acc[...], g_ref[...])


def _peer_main(xn, u, v, lim, coef, rank2, e2, h2d, g, tn, eb):
    n, d = h2d.shape
    e = u.shape[0]
    assert n % tn == 0 and e % eb == 0 and eb % PEER_NKEYS == 0
    sel = pl.BlockSpec((PEER_HEADS, PEER_NKEYS, tn), lambda i, j: (0, 0, i))
    tok = pl.BlockSpec((tn, d), lambda i, j: (i, 0))
    tab = pl.BlockSpec((eb, d), lambda i, j: (j, 0))
    vmem = 2 * tn * d * 2 + 4 * eb * d * 2 + 6 * PEER_HEADS * PEER_NKEYS * tn * 4 + 5 * tn * d * 4 + (8 << 20)
    return pl.pallas_call(
        _peer_main_kernel,
        grid=(n // tn, e // eb),
        in_specs=[tok, tab, tab, sel, sel, sel, sel, tok, _resident((1, d))],
        out_specs=tok,
        out_shape=jax.ShapeDtypeStruct((n, d), F32),
        scratch_shapes=[pltpu.VMEM((tn, d), F32)],
        compiler_params=_cparams(("parallel", "arbitrary"), vmem),
        name="peer_main",
    )(xn, u, v, lim, coef, rank2, e2, h2d, g.reshape(1, d))


def _to_chain_major(x, heads):
    b, t, c = x.shape
    n = c // heads
    return x.reshape(b, t, heads, n).transpose(1, 3, 0, 2).reshape(t, n, b * heads)


def _from_chain_major(x, b):
    t, n, ch = x.shape
    heads = ch // b
    return x.reshape(t, n, b, heads).transpose(2, 0, 3, 1).reshape(b, t, heads * n)


def _head_param(p, heads, b):
    n = p.size // heads
    return jnp.tile(p.reshape(heads, n).T, (1, b))


def _layer(x, past_k, past_v, wkv0, shift0, wts, sb_hp, rw_tt, peer_tn, peer_eb):
    b, t, d = x.shape
    n_tok = b * t
    x2d = x.reshape(n_tok, d)
    sbw = wts["wq"].shape[1]
    rww = wts["w0"].shape[-1]
    rw_heads = rww // RW_HEAD_DIM
    proj = wts["w_rw"].shape[1]

    q, k, v, gate_sb, gate_rw = _norm_matmul(
        x2d, wts["norm1_g"], [wts["wq"], wts["wk"], wts["wv"], wts["w_gsb"], wts["w_grw"]])

    q3, k3, v3 = (a.reshape(b, t, sbw) for a in (q, k, v))
    heads_sb = sbw // SB_HEAD_DIM
    if past_k is None:
        o_sb = _sb_attention(q3, k3, v3, k3, v3, prompt=True, hp=sb_hp)
        wkv0 = jnp.zeros((b, rw_heads, RW_HEAD_DIM, RW_HEAD_DIM), F32)
        shift0 = jnp.zeros((b, 1, proj), F32)
    else:
        o_sb = _sb_attention(q3, k3, v3, past_k, past_v, prompt=False, hp=sb_hp)

    r, w, kr, vr, a, g, shift_new = _rw_project(x2d, wts["norm1_g"], wts["w_rw"], shift0, wts["rw_mu"], wts["w0"],
                                                wts["a0"], wts["wa2"], wts["g2"], seq_len=t, tm=min(t, 256))
    seqs = [_to_chain_major(s, rw_heads) for s in (r, w, kr, vr, a)]
    params = [_head_param(wts[nm], rw_heads, b) for nm in ("k_k", "k_a", "r_k", "ln_g", "ln_b")]
    s0 = wkv0.astype(F32).transpose(3, 2, 0, 1).reshape(RW_HEAD_DIM, RW_HEAD_DIM, b * rw_heads)
    oT, sT = _rw_scan(*seqs, params, s0, tt=rw_tt)
    o_rw = _from_chain_major(oT, b).reshape(n_tok, rww)
    wkv_new = sT.reshape(RW_HEAD_DIM, RW_HEAD_DIM, b, rw_heads).transpose(2, 3, 1, 0)

    h = _mix(o_sb.reshape(n_tok, sbw), o_rw, g.reshape(n_tok, rww), gate_sb, gate_rw, x2d,
             wts["w_bsb"], wts["w_brw"], wts["w_out"])
    xn2, lim, coef, rank2, e2 = _peer_select(h, wts["norm2_g"], wts["peer_wq"], wts["peer_k1"], wts["peer_k2"], tn=256)
    y = _peer_main(xn2, wts["peer_u"], wts["peer_v"], lim, coef, rank2, e2, h, wts["final_g"], tn=peer_tn, eb=peer_eb)
    kv_shape = (b, t, heads_sb, SB_HEAD_DIM)
    return y.reshape(b, t, d), k.reshape(kv_shape), v.reshape(kv_shape), wkv_new, shift_new


def kernel(x_prompt, x_sample, cache_sb_k, cache_sb_v, state_rwkv_wkv, state_rwkv_shift, norm1_g, w_in, rw_mu, rw_w0, rw_w2, rw_a0, rw_a2, rw_g2, rw_k_k, rw_k_a, rw_r_k, rw_ln_g, rw_ln_b, w_branch_sb, w_branch_rw, w_out, norm2_g, peer_wq, peer_k1, peer_k2, peer_u, peer_v, final_norm_g):
    assert norm1_g.shape[0] == 1, "single-layer stack"
    d = x_prompt.shape[-1]
    sbw = w_branch_sb.shape[1]
    rww = w_branch_rw.shape[1]
    proj = rw_mu.shape[1]
    w = w_in[0]
    o = 0
    cols = {}
    for nm, width in (("wq", sbw), ("wk", sbw), ("wv", sbw), ("w_rw", proj), ("w_gsb", d), ("w_grw", d)):
        cols[nm] = w[:, o:o + width].astype(BF16)
        o += width
    zeros = jnp.zeros((RW_DECAY_RANK, rww), F32)
    wa2 = jnp.concatenate([jnp.concatenate([rw_w2[0], zeros], axis=1),
                           jnp.concatenate([zeros, rw_a2[0]], axis=1)], axis=0).astype(BF16)
    wts = dict(cols, norm1_g=norm1_g[0], rw_mu=rw_mu[0], w0=rw_w0[0], a0=rw_a0[0], wa2=wa2, g2=rw_g2[0].astype(BF16),
               k_k=rw_k_k[0], k_a=rw_k_a[0], r_k=rw_r_k[0].reshape(-1), ln_g=rw_ln_g[0], ln_b=rw_ln_b[0],
               w_bsb=w_branch_sb[0].astype(BF16), w_brw=w_branch_rw[0].astype(BF16), w_out=w_out[0].astype(BF16),
               norm2_g=norm2_g[0], peer_wq=peer_wq[0].astype(BF16), peer_k1=peer_k1[0], peer_k2=peer_k2[0],
               peer_u=peer_u[0].astype(BF16), peer_v=peer_v[0].astype(BF16), final_g=final_norm_g)

    sb_heads = sbw // SB_HEAD_DIM
    yp, kp, vp, wp, sp = _layer(x_prompt, None, None, None, None, wts, sb_hp=4, rw_tt=32, peer_tn=512, peer_eb=1024)
    ys, ks, vs, ws, ss = _layer(x_sample, cache_sb_k[0], cache_sb_v[0], state_rwkv_wkv[0], state_rwkv_shift[0], wts,
                                sb_hp=sb_heads, rw_tt=32, peer_tn=512, peer_eb=1024)
    st = lambda t: t[None]
    return (yp, ys, st(kp), st(vp), st(wp.astype(x_prompt.dtype)), st(sp), st(ks), st(vs), st(ws.astype(x_sample.dtype)), st(ss))
```

```python
import functools

import jax
import jax.numpy as jnp
from jax import lax
from jax.experimental import pallas as pl
from jax.experimental.pallas import tpu as pltpu

F32 = jnp.float32
BF16 = jnp.bfloat16

RMS_EPS = 1e-6
GN_EPS = 64e-5
SB_HEAD_DIM = 128
RW_HEAD_DIM = 64
RW_DECAY_RANK = 64
RW_A_RANK = 64
PEER_HEADS = 8
PEER_NKEYS = 128
PEER_HALF = 128
PEER_TOPK = 16

V7X_LANES = 128
V7X_SUBLANES = 8
V7X_VMEM_LIMIT_BYTES = 60000 * 1024

NEG_BIG = -1e30


def _cparams(semantics, vmem_bytes):
    return pltpu.CompilerParams(dimension_semantics=semantics,
                                vmem_limit_bytes=int(min(V7X_VMEM_LIMIT_BYTES, vmem_bytes)))


def _sigmoid(x):
    return 1.0 / (1.0 + jnp.exp(-x))


def _softplus(x):
    return jnp.maximum(x, 0.0) + jnp.log1p(jnp.exp(-jnp.abs(x)))


def _rms(x, g):
    return x * lax.rsqrt(jnp.mean(x * x, axis=-1, keepdims=True) + RMS_EPS) * g


def _dot(a, b):
    return jnp.dot(a, b, preferred_element_type=F32)


def _dot_nt(a, b):
    return lax.dot_general(a, b, (((1,), (1,)), ((), ())), preferred_element_type=F32)


def _dot_tn(a, b):
    return lax.dot_general(a, b, (((0,), (0,)), ((), ())), preferred_element_type=F32)


def _resident(shape):
    nd = len(shape)
    return pl.BlockSpec(shape, lambda *_: (0,) * nd, pipeline_mode=pl.Buffered(1))


def _norm_matmul_kernel(n_w, x_ref, g_ref, *refs):
    xn = _rms(x_ref[...], g_ref[...]).astype(BF16)
    for w_ref, o_ref in zip(refs[:n_w], refs[n_w:]):
        o_ref[...] = _dot(xn, w_ref[...])


def _norm_matmul(x2d, g, ws, tm=256):
    n, d = x2d.shape
    assert n % tm == 0
    wbytes = sum(w.size * 2 for w in ws)
    obytes = sum(2 * tm * w.shape[1] * 4 for w in ws)
    return pl.pallas_call(
        functools.partial(_norm_matmul_kernel, len(ws)),
        grid=(n // tm,),
        in_specs=[pl.BlockSpec((tm, d), lambda i: (i, 0)), _resident((1, d))] + [_resident(w.shape) for w in ws],
        out_specs=[pl.BlockSpec((tm, w.shape[1]), lambda i: (i, 0)) for w in ws],
        out_shape=[jax.ShapeDtypeStruct((n, w.shape[1]), F32) for w in ws],
        compiler_params=_cparams(("parallel",), wbytes + 2 * obytes + 6 * tm * d * 4 + (8 << 20)),
        name="norm_matmul",
    )(x2d, g.reshape(1, d), *ws)


SB_SUPER = 512
SB_SUB = 256
LOG2E = 1.4426950408889634
SB_DEAD_LOG2 = 160.0


def _sb_kernel(n_past_static, hp, q_ref, kd_ref, vd_ref, kp_ref, vp_ref, o_ref, acc_s, q_s):
    bq = q_ref.shape[0]
    dh = SB_HEAD_DIM
    q_s[...] = (q_ref[...] * (dh ** -0.5 * LOG2E)).astype(BF16)

    def neg_tri(n):
        r = lax.broadcasted_iota(jnp.int32, (n, n), 0)
        c = lax.broadcasted_iota(jnp.int32, (n, n), 1)
        return jnp.where(r > c, -1.0, 0.0).astype(BF16)

    def superblock(h, kb, vb, c, masked, first):
        cols = slice(h * dh, (h + 1) * dh)
        ks = kb.shape[0]
        sub = min(SB_SUB, ks)
        nsub = ks // sub
        z = _dot_nt(q_s[:, cols], kb.astype(BF16))
        sp = jnp.maximum(z, 0.0) + jnp.log2(1.0 + jnp.exp2(-jnp.abs(z)))
        if masked:
            allowed = lax.broadcasted_iota(jnp.int32, (bq, ks), 1) < lax.broadcasted_iota(jnp.int32, (bq, ks), 0)
            sp = jnp.where(allowed, sp, 0.0)
        sp16 = sp.astype(BF16)
        stacked = jnp.concatenate([sp16[:, k * sub:(k + 1) * sub] for k in range(nsub)], axis=0)
        after = _dot(stacked, neg_tri(sub))
        ws = [None] * nsub
        for k in reversed(range(nsub)):
            kc = slice(k * sub, (k + 1) * sub)
            w = jnp.exp2(z[:, kc] - sp[:, kc] + after[k * bq:(k + 1) * bq] + c)
            if masked:
                w = jnp.where(allowed[:, kc], w, 0.0)
            ws[k] = w.astype(BF16)
            c = c - jnp.sum(sp[:, kc], axis=1, keepdims=True)
        pv = _dot(jnp.concatenate(ws, axis=1), vb.astype(BF16))
        if first:
            acc_s[:, cols] = pv
        else:
            acc_s[:, cols] += pv
        return c

    cs = tuple(superblock(h, kd_ref[:, h * dh:(h + 1) * dh], vd_ref[:, h * dh:(h + 1) * dh],
                          jnp.zeros((bq, 1), F32), True, True) for h in range(hp))

    n_past = pl.program_id(2) if n_past_static is None else n_past_static

    def body(it, cs):
        rows = pl.ds(pl.multiple_of((n_past - 1 - it) * SB_SUPER, SB_SUPER), SB_SUPER)
        if len(kp_ref.shape) == 2:
            kbs = [kp_ref[rows, h * dh:(h + 1) * dh] for h in range(hp)]
            vbs = [vp_ref[rows, h * dh:(h + 1) * dh] for h in range(hp)]
        else:
            kbs = pltpu.einshape("thd->htd", kp_ref[rows])
            vbs = pltpu.einshape("thd->htd", vp_ref[rows])
        return tuple(superblock(h, kbs[h], vbs[h], cs[h], False, False) for h in range(hp))

    def live(cs):
        top = cs[0]
        for c in cs[1:]:
            top = jnp.maximum(top, c)
        return jnp.max(top) > -SB_DEAD_LOG2

    def w_cond(state):
        it, alive, _ = state
        return jnp.logical_and(it < n_past, alive)

    def w_body(state):
        it, _, cs = state
        cs = body(it, cs)
        return it + 1, live(cs), cs

    lax.while_loop(w_cond, w_body, (jnp.int32(0), live(cs), cs))
    o_ref[...] = acc_s[...]


def _sb_attention(q, kd, vd, kp, vp, prompt, hp):
    b, tq, width = q.shape
    heads = width // SB_HEAD_DIM
    tp = kp.shape[1]
    assert tp % SB_SUPER == 0 and heads % hp == 0
    if prompt:
        bq, n_past = SB_SUPER, None
    else:
        bq, n_past = tq, tp // SB_SUPER
    assert tq % bq == 0
    gw = hp * SB_HEAD_DIM
    qspec = pl.BlockSpec((None, bq, gw), lambda bi, h, i: (bi, i, h))
    if kp.ndim == 4:
        assert hp == heads
        pspec = pl.BlockSpec((None, tp, heads, SB_HEAD_DIM), lambda bi, h, i: (bi, 0, 0, 0))
    else:
        pspec = pl.BlockSpec((None, tp, gw), lambda bi, h, i: (bi, 0, h))
    return pl.pallas_call(
        functools.partial(_sb_kernel, n_past, hp),
        grid=(b, heads // hp, tq // bq),
        in_specs=[qspec, qspec, qspec, pspec, pspec],
        out_specs=qspec,
        out_shape=jax.ShapeDtypeStruct((b, tq, width), F32),
        scratch_shapes=[pltpu.VMEM((bq, gw), F32), pltpu.VMEM((bq, gw), BF16)],
        compiler_params=_cparams(("parallel", "parallel", "arbitrary"),
                                 4 * tp * gw * 4 + 10 * bq * gw * 4 + 12 * bq * SB_SUPER * 4 * hp + (8 << 20)),
        name="sb_attention",
    )(q, kd, vd, kp, vp)


def _rw_project_kernel(tiles_per_seq, rw_width, x_ref, g_ref, wrw_ref, sh_ref, mu_ref, w0_ref, a0_ref, wa2_ref, g2_ref,
                       r_o, w_o, k_o, v_o, a_o, g_o, last_o, carry):
    @pl.when(pl.program_id(0) % tiles_per_seq == 0)
    def _():
        carry[...] = sh_ref[...]

    xn = _rms(x_ref[...], g_ref[...]).astype(BF16)
    c2 = 2 * rw_width
    p_hi = _dot(xn, wrw_ref[:, c2:])
    p_lo = _dot(xn, wrw_ref[:, :c2])
    p = jnp.concatenate([p_lo, p_hi], axis=1)
    tm = p.shape[0]
    row = lax.broadcasted_iota(jnp.int32, p.shape, 0)
    prev = jnp.where(row == 0, carry[...], pltpu.roll(p, 1, axis=0))
    carry[...] = p[tm - 1:tm, :]
    last_o[...] = p[tm - 1:tm, :]
    ps = p + (prev - p) * mu_ref[...]
    c = rw_width
    r_o[...] = ps[:, 0:c]
    k_o[...] = ps[:, c:2 * c]
    v_o[...] = ps[:, 2 * c:3 * c]
    xwa = ps[:, 3 * c:3 * c + RW_DECAY_RANK + RW_A_RANK]
    lane = lax.broadcasted_iota(jnp.int32, xwa.shape, 1)
    lowrank_in = jnp.where(lane < RW_DECAY_RANK, jnp.tanh(xwa), xwa).astype(BF16)
    wa = _dot(lowrank_in, wa2_ref[...])
    w = -_softplus(-(w0_ref[...] + wa[:, 0:c])) - 0.5
    w_o[...] = jnp.exp(-jnp.exp(w))
    a_o[...] = _sigmoid(a0_ref[...] + wa[:, c:2 * c])
    xg = ps[:, 3 * c + RW_DECAY_RANK + RW_A_RANK:]
    g_o[...] = _dot(_sigmoid(xg).astype(BF16), g2_ref[...])


def _rw_project(x2d, norm_g, w_rw, shift0, mu, w0, a0, wa2, g2, seq_len, tm):
    n, d = x2d.shape
    proj = w_rw.shape[1]
    c = w0.shape[-1]
    assert seq_len % tm == 0 and n % seq_len == 0
    tps = seq_len // tm
    tok = pl.BlockSpec((None, tm, c), lambda i: (i // tps, i % tps, 0))
    per_seq = pl.BlockSpec((None, 1, proj), lambda i: (i // tps, 0, 0))
    return pl.pallas_call(
        functools.partial(_rw_project_kernel, tps, c),
        grid=(n // tm,),
        in_specs=[pl.BlockSpec((tm, d), lambda i: (i, 0)), _resident((1, d)), _resident(w_rw.shape), per_seq,
                  _resident((1, proj)), _resident((1, c)), _resident((1, c)), _resident(wa2.shape), _resident(g2.shape)],
        out_specs=[tok] * 6 + [per_seq],
        out_shape=[jax.ShapeDtypeStruct((n // seq_len, seq_len, c), F32)] * 6
        + [jax.ShapeDtypeStruct((n // seq_len, 1, proj), F32)],
        scratch_shapes=[pltpu.VMEM((1, proj), F32)],
        compiler_params=_cparams(("arbitrary",), w_rw.size * 2 + 6 * tm * proj * 4 + 14 * tm * c * 4 + 4 * tm * d * 4 + (8 << 20)),
        name="rw_project",
    )(x2d, norm_g.reshape(1, d), w_rw, shift0, mu.reshape(1, proj), w0.reshape(1, c), a0.reshape(1, c), wa2, g2)


def _rw_scan_kernel(r_ref, w_ref, k_ref, v_ref, a_ref, kk_p, ka_p, rk_p, lng_p, lnb_p, s0_ref,
                    o_ref, S, kk_s, b_s, kt_s, r_s, bonus_s, g_s):
    j = pl.program_id(1)
    nk = S.shape[0]
    tt = r_ref.shape[0]

    @pl.when(j == 0)
    def _():
        S[...] = s0_ref[...]

    def prep(t, g):
        r = r_ref[t]
        k = k_ref[t]
        a = a_ref[t]
        kk = k * kk_p[...]
        nrm = jnp.sqrt(jnp.sum(kk * kk, axis=0, keepdims=True))
        kk = kk / jnp.maximum(nrm, 1e-12)
        kt = k * (1.0 + (a - 1.0) * ka_p[...])
        g_new = g * w_ref[t]
        inv_g = 1.0 / g_new
        kk_s[t] = kk * g
        b_s[t] = (kk * a) * inv_g
        kt_s[t] = kt * inv_g
        r_s[t] = r * g_new
        bonus_s[t] = jnp.sum(r * kt * rk_p[...], axis=0, keepdims=True) * v_ref[t]
        return g_new

    g_s[...] = lax.fori_loop(0, tt, prep, jnp.ones(g_s.shape, F32), unroll=2)

    def step(t, carry):
        v = v_ref[t]

        def p1(kx, sa):
            return sa + S[kx] * kk_s[t, pl.ds(kx, 1), :]

        sa = lax.fori_loop(0, nk, p1, jnp.zeros(v.shape, F32), unroll=16)

        def p2(kx, y):
            row = pl.ds(kx, 1)
            s_new = S[kx] - sa * b_s[t, row, :] + v * kt_s[t, row, :]
            S[kx] = s_new
            return y + s_new * r_s[t, row, :]

        o_ref[t] = lax.fori_loop(0, nk, p2, jnp.zeros(v.shape, F32), unroll=16)
        return carry

    lax.fori_loop(0, tt, step, 0)

    def post(t, carry):
        y = o_ref[t]
        mean = jnp.mean(y, axis=0, keepdims=True)
        d = y - mean
        var = jnp.mean(d * d, axis=0, keepdims=True)
        o_ref[t] = d * lax.rsqrt(var + GN_EPS) * lng_p[...] + lnb_p[...] + bonus_s[t]
        return carry

    lax.fori_loop(0, tt, post, 0, unroll=2)

    def fold(kx, carry):
        S[kx] = S[kx] * g_s[pl.ds(kx, 1), :]
        return carry

    lax.fori_loop(0, nk, fold, 0, unroll=16)


def _rw_scan(rT, wT, kT, vT, aT, params, s0, tt):
    t, n, ctot = rT.shape
    assert t % tt == 0 and ctot % V7X_LANES == 0
    seq = pl.BlockSpec((tt, n, V7X_LANES), lambda c, j: (j, 0, c))
    par = pl.BlockSpec((n, V7X_LANES), lambda c, j: (0, c))
    st = pl.BlockSpec((n, n, V7X_LANES), lambda c, j: (0, 0, c))
    return pl.pallas_call(
        _rw_scan_kernel,
        grid=(ctot // V7X_LANES, t // tt),
        in_specs=[seq] * 5 + [par] * 5 + [st],
        out_specs=[seq, st],
        out_shape=[jax.ShapeDtypeStruct((t, n, ctot), F32), jax.ShapeDtypeStruct((n, n, ctot), F32)],
        scratch_shapes=[pltpu.VMEM((tt, n, V7X_LANES), F32)] * 5 + [pltpu.VMEM((n, V7X_LANES), F32)],
        compiler_params=_cparams(("parallel", "arbitrary"), 17 * tt * n * V7X_LANES * 4 + 5 * n * n * V7X_LANES * 4 + (8 << 20)),
        name="rw_scan",
    )(rT, wT, kT, vT, aT, *params, s0)


def _mix_kernel(osb_ref, orw_ref, g_ref, gsb_ref, grw_ref, x_ref, wsb_ref, wrw_ref, wout_ref, h_ref):
    sb = _dot(osb_ref[...].astype(BF16), wsb_ref[...])
    rw = _dot((orw_ref[...] * g_ref[...]).astype(BF16), wrw_ref[...])
    mixed = _sigmoid(gsb_ref[...]) * sb + _sigmoid(grw_ref[...]) * rw
    h_ref[...] = x_ref[...] + _dot(mixed.astype(BF16), wout_ref[...])


def _mix(o_sb, o_rw, g, gate_sb, gate_rw, x2d, wsb, wrw, wout, tm=256):
    n, d = x2d.shape
    c = o_sb.shape[1]
    assert n % tm == 0
    half = pl.BlockSpec((tm, c), lambda i: (i, 0))
    full = pl.BlockSpec((tm, d), lambda i: (i, 0))
    return pl.pallas_call(
        _mix_kernel,
        grid=(n // tm,),
        in_specs=[half, half, half, full, full, full, _resident(wsb.shape), _resident(wrw.shape), _resident(wout.shape)],
        out_specs=full,
        out_shape=jax.ShapeDtypeStruct((n, d), F32),
        compiler_params=_cparams(("parallel",), (wsb.size + wrw.size + wout.size) * 2 + 2 * tm * (3 * c + 4 * d) * 4 + (12 << 20)),
        name="branch_mix",
    )(o_sb, o_rw, g, gate_sb, gate_rw, x2d, wsb, wrw, wout)


PEER_RANKS = PEER_TOPK + 1
PEER_RANK_ROWS = 24
PEER_UNRANKED = 127.0


def _top_ranks(s, t_ref, want_ranks):
    t_ref[...] = jnp.full(t_ref.shape, NEG_BIG, F32)
    ranks = jnp.full(s.shape, PEER_UNRANKED, F32) if want_ranks else None
    for rnk in range(PEER_RANKS):
        m = jnp.max(s, axis=0, keepdims=True)
        t_ref[rnk:rnk + 1, :] = m
        hit = s == m
        if want_ranks:
            ranks = jnp.where(hit, float(rnk), ranks)
        s = jnp.where(hit, NEG_BIG, s)
    return ranks


def _peer_select_kernel(h_ref, g_ref, wq_ref, k1_ref, k2_ref, xn_o, lim_o, coef_o, rank2_o, e2_o, t1_s, t2_s):
    xn = _rms(h_ref[...], g_ref[...]).astype(BF16)
    xn_o[...] = xn
    q = _dot(xn, wq_ref[...])
    k1 = k1_ref[...].astype(BF16)
    k2 = k2_ref[...].astype(BF16)
    sub = V7X_SUBLANES

    for h in range(PEER_HEADS):
        base = 2 * PEER_HALF * h
        s1 = _dot_nt(k1, q[:, base:base + PEER_HALF].astype(BF16))
        s2 = _dot_nt(k2, q[:, base + PEER_HALF:base + 2 * PEER_HALF].astype(BF16))
        _top_ranks(s1, t1_s, False)
        rank2 = _top_ranks(s2, t2_s, True)
        t1 = t1_s[...]
        t2 = t2_s[...]
        cand = jnp.concatenate(
            [t1[0:1] + t2[0:sub], t1[0:1] + t2[sub:2 * sub], t1[0:1] + t2[2 * sub:3 * sub],
             t2[0:1] + t1[0:sub], t2[0:1] + t1[sub:2 * sub], t2[0:1] + t1[2 * sub:3 * sub],
             t1[1:2] + t2[0:sub], t2[1:2] + t1[0:sub],
             t1[2:3] + t2[0:sub], t1[3:4] + t2[0:sub], t1[4:5] + t2[0:sub]], axis=0)
        top = jnp.max(cand, axis=0, keepdims=True)
        zsum = jnp.zeros(top.shape, F32)
        lo16 = top
        for rnk in range(PEER_RANKS):
            m = jnp.max(cand, axis=0, keepdims=True)
            if rnk < PEER_TOPK:
                zsum = zsum + jnp.exp(m - top)
                lo16 = m
            else:
                tau = 0.5 * (lo16 + m)
            cand = jnp.where(cand == m, NEG_BIG, cand)
        lim = jnp.zeros(s1.shape, F32)
        for a in range(PEER_TOPK):
            width = jnp.sum(jnp.where(t2 > tau - t1[a:a + 1], 1.0, 0.0), axis=0, keepdims=True)
            lim = jnp.where(s1 == t1[a:a + 1], width, lim)
        lim_o[h] = lim
        coef_o[h] = jnp.exp(s1 - t1[0:1]) / zsum
        rank2_o[h] = rank2.astype(BF16)
        e2_o[h] = jnp.exp(s2 - t2[0:1]).astype(BF16)


def _peer_select(h2d, g, wq, k1, k2, tn):
    n, d = h2d.shape
    assert n % tn == 0
    sel = pl.BlockSpec((PEER_HEADS, PEER_NKEYS, tn), lambda i: (0, 0, i))
    sel_f32 = jax.ShapeDtypeStruct((PEER_HEADS, PEER_NKEYS, n), F32)
    sel_bf16 = jax.ShapeDtypeStruct((PEER_HEADS, PEER_NKEYS, n), BF16)
    return pl.pallas_call(
        _peer_select_kernel,
        grid=(n // tn,),
        in_specs=[pl.BlockSpec((tn, d), lambda i: (i, 0)), _resident((1, d)), _resident(wq.shape),
                  _resident(k1.shape), _resident(k2.shape)],
        out_specs=[pl.BlockSpec((tn, d), lambda i: (i, 0)), sel, sel, sel, sel],
        out_shape=[jax.ShapeDtypeStruct((n, d), BF16), sel_f32, sel_f32, sel_bf16, sel_bf16],
        scratch_shapes=[pltpu.VMEM((PEER_RANK_ROWS, tn), F32)] * 2,
        compiler_params=_cparams(("parallel",), wq.size * 2 + 8 * PEER_HEADS * PEER_NKEYS * tn * 4 + 8 * tn * d * 4 + (12 << 20)),
        name="peer_select",
    )(h2d, g.reshape(1, d), wq, k1, k2)


def _peer_main_kernel(xn_ref, u_ref, v_ref, lim_ref, coef_ref, rank2_ref, e2_ref, h_ref, g_ref, y_ref, acc):
    j = pl.program_id(1)
    eb = u_ref.shape[0]
    nib = eb // PEER_NKEYS
    tile = (PEER_NKEYS, xn_ref.shape[0])

    @pl.when(j == 0)
    def _():
        acc[...] = jnp.zeros(acc.shape, F32)

    hid = _dot_nt(u_ref[...], xn_ref[...])
    parts = []
    for ii in range(nib):
        row = pl.ds(j * nib + ii, 1)
        wgt = jnp.zeros(tile, BF16)
        for h in range(PEER_HEADS):
            lim = jnp.broadcast_to(lim_ref[h, row, :], tile).astype(BF16)
            coef = jnp.broadcast_to(coef_ref[h, row, :], tile).astype(BF16)
            picked = jnp.where(rank2_ref[h] < lim, e2_ref[h], jnp.zeros((), BF16))
            wgt = wgt + coef * picked
        x = hid[ii * PEER_NKEYS:(ii + 1) * PEER_NKEYS, :].astype(BF16)
        gelu = (0.5 * x) * (1.0 + lax.erf(x * (2.0 ** -0.5)))
        parts.append(gelu * wgt)
    gated = jnp.concatenate(parts, axis=0)
    acc[...] += _dot_tn(gated, v_ref[...])

    @pl.when(j == pl.num_programs(1) - 1)
    def _():
        y_ref[...] = _rms(h_ref[...] + acc[...], g_ref[...])


def _peer_main(xn, u, v, lim, coef, rank2, e2, h2d, g, tn, eb):
    n, d = h2d.shape
    e = u.shape[0]
    assert n % tn == 0 and e % eb == 0 and eb % PEER_NKEYS == 0
    sel = pl.BlockSpec((PEER_HEADS, PEER_NKEYS, tn), lambda i, j: (0, 0, i))
    tok = pl.BlockSpec((tn, d), lambda i, j: (i, 0))
    tab = pl.BlockSpec((eb, d), lambda i, j: (j, 0))
    vmem = 2 * tn * d * 2 + 4 * eb * d * 2 + 6 * PEER_HEADS * PEER_NKEYS * tn * 4 + 5 * tn * d * 4 + (8 << 20)
    return pl.pallas_call(
        _peer_main_kernel,
        grid=(n // tn, e // eb),
        in_specs=[tok, tab, tab, sel, sel, sel, sel, tok, _resident((1, d))],
        out_specs=tok,
        out_shape=jax.ShapeDtypeStruct((n, d), F32),
        scratch_shapes=[pltpu.VMEM((tn, d), F32)],
        compiler_params=_cparams(("parallel", "arbitrary"), vmem),
        name="peer_main",
    )(xn, u, v, lim, coef, rank2, e2, h2d, g.reshape(1, d))


def _to_chain_major(x, heads):
    b, t, c = x.shape
    n = c // heads
    return x.reshape(b, t, heads, n).transpose(1, 3, 0, 2).reshape(t, n, b * heads)


def _from_chain_major(x, b):
    t, n, ch = x.shape
    heads = ch // b
    return x.reshape(t, n, b, heads).transpose(2, 0, 3, 1).reshape(b, t, heads * n)


def _head_param(p, heads, b):
    n = p.size // heads
    return jnp.tile(p.reshape(heads, n).T, (1, b))


def _layer(x, past_k, past_v, wkv0, shift0, wts, sb_hp, rw_tt, peer_tn, peer_eb):
    b, t, d = x.shape
    n_tok = b * t
    x2d = x.reshape(n_tok, d)
    sbw = wts["wq"].shape[1]
    rww = wts["w0"].shape[-1]
    rw_heads = rww // RW_HEAD_DIM
    proj = wts["w_rw"].shape[1]

    q, k, v, gate_sb, gate_rw = _norm_matmul(
        x2d, wts["norm1_g"], [wts["wq"], wts["wk"], wts["wv"], wts["w_gsb"], wts["w_grw"]])

    q3, k3, v3 = (a.reshape(b, t, sbw) for a in (q, k, v))
    heads_sb = sbw // SB_HEAD_DIM
    if past_k is None:
        o_sb = _sb_attention(q3, k3, v3, k3, v3, prompt=True, hp=sb_hp)
        wkv0 = jnp.zeros((b, rw_heads, RW_HEAD_DIM, RW_HEAD_DIM), F32)
        shift0 = jnp.zeros((b, 1, proj), F32)
    else:
        o_sb = _sb_attention(q3, k3, v3, past_k, past_v, prompt=False, hp=sb_hp)

    r, w, kr, vr, a, g, shift_new = _rw_project(x2d, wts["norm1_g"], wts["w_rw"], shift0, wts["rw_mu"], wts["w0"],
                                                wts["a0"], wts["wa2"], wts["g2"], seq_len=t, tm=min(t, 256))
    seqs = [_to_chain_major(s, rw_heads) for s in (r, w, kr, vr, a)]
    params = [_head_param(wts[nm], rw_heads, b) for nm in ("k_k", "k_a", "r_k", "ln_g", "ln_b")]
    s0 = wkv0.astype(F32).transpose(3, 2, 0, 1).reshape(RW_HEAD_DIM, RW_HEAD_DIM, b * rw_heads)
    oT, sT = _rw_scan(*seqs, params, s0, tt=rw_tt)
    o_rw = _from_chain_major(oT, b).reshape(n_tok, rww)
    wkv_new = sT.reshape(RW_HEAD_DIM, RW_HEAD_DIM, b, rw_heads).transpose(2, 3, 1, 0)

    h = _mix(o_sb.reshape(n_tok, sbw), o_rw, g.reshape(n_tok, rww), gate_sb, gate_rw, x2d,
             wts["w_bsb"], wts["w_brw"], wts["w_out"])
    xn2, lim, coef, rank2, e2 = _peer_select(h, wts["norm2_g"], wts["peer_wq"], wts["peer_k1"], wts["peer_k2"], tn=256)
    y = _peer_main(xn2, wts["peer_u"], wts["peer_v"], lim, coef, rank2, e2, h, wts["final_g"], tn=peer_tn, eb=peer_eb)
    kv_shape = (b, t, heads_sb, SB_HEAD_DIM)
    return y.reshape(b, t, d), k.reshape(kv_shape), v.reshape(kv_shape), wkv_new, shift_new


def kernel(x_prompt, x_sample, cache_sb_k, cache_sb_v, state_rwkv_wkv, state_rwkv_shift, norm1_g, w_in, rw_mu, rw_w0, rw_w2, rw_a0, rw_a2, rw_g2, rw_k_k, rw_k_a, rw_r_k, rw_ln_g, rw_ln_b, w_branch_sb, w_branch_rw, w_out, norm2_g, peer_wq, peer_k1, peer_k2, peer_u, peer_v, final_norm_g):
    assert norm1_g.shape[0] == 1, "single-layer stack"
    d = x_prompt.shape[-1]
    sbw = w_branch_sb.shape[1]
    rww = w_branch_rw.shape[1]
    proj = rw_mu.shape[1]
    w = w_in[0]
    o = 0
    cols = {}
    for nm, width in (("wq", sbw), ("wk", sbw), ("wv", sbw), ("w_rw", proj), ("w_gsb", d), ("w_grw", d)):
        cols[nm] = w[:, o:o + width].astype(BF16)
        o += width
    zeros = jnp.zeros((RW_DECAY_RANK, rww), F32)
    wa2 = jnp.concatenate([jnp.concatenate([rw_w2[0], zeros], axis=1),
                           jnp.concatenate([zeros, rw_a2[0]], axis=1)], axis=0).astype(BF16)
    wts = dict(cols, norm1_g=norm1_g[0], rw_mu=rw_mu[0], w0=rw_w0[0], a0=rw_a0[0], wa2=wa2, g2=rw_g2[0].astype(BF16),
               k_k=rw_k_k[0], k_a=rw_k_a[0], r_k=rw_r_k[0].reshape(-1), ln_g=rw_ln_g[0], ln_b=rw_ln_b[0],
               w_bsb=w_branch_sb[0].astype(BF16), w_brw=w_branch_rw[0].astype(BF16), w_out=w_out[0].astype(BF16),
               norm2_g=norm2_g[0], peer_wq=peer_wq[0].astype(BF16), peer_k1=peer_k1[0], peer_k2=peer_k2[0],
               peer_u=peer_u[0].astype(BF16), peer_v=peer_v[0].astype(BF16), final_g=final_norm_g)

    sb_heads = sbw // SB_HEAD_DIM
    yp, kp, vp, wp, sp = _layer(x_prompt, None, None, None, None, wts, sb_hp=4, rw_tt=32, peer_tn=512, peer_eb=1024)
    ys, ks, vs, ws, ss = _layer(x_sample, cache_sb_k[0], cache_sb_v[0], state_rwkv_wkv[0], state_rwkv_shift[0], wts,
                                sb_hp=sb_heads, rw_tt=32, peer_tn=512, peer_eb=1024)
    st = lambda t: t[None]
    return (yp, ys, st(kp), st(vp), st(wp.astype(x_prompt.dtype)), st(sp), st(ks), st(vs), st(ws.astype(x_sample.dtype)), st(ss))
```

```python
import functools

import jax
import jax.numpy as jnp
from jax import lax
from jax.experimental import pallas as pl
from jax.experimental.pallas import tpu as pltpu

F32 = jnp.float32
BF16 = jnp.bfloat16

RMS_EPS = 1e-6
GN_EPS = 64e-5
SB_HEAD_DIM = 128
RW_HEAD_DIM = 64
RW_DECAY_RANK = 64
RW_A_RANK = 64
PEER_HEADS = 8
PEER_NKEYS = 128
PEER_HALF = 128
PEER_TOPK = 16

V7X_LANES = 128
V7X_SUBLANES = 8
V7X_VMEM_LIMIT_BYTES = 60000 * 1024

NEG_BIG = -1e30


def _cparams(semantics, vmem_bytes):
    return pltpu.CompilerParams(dimension_semantics=semantics,
                                vmem_limit_bytes=int(min(V7X_VMEM_LIMIT_BYTES, vmem_bytes)))


def _sigmoid(x):
    return 1.0 / (1.0 + jnp.exp(-x))


def _softplus(x):
    return jnp.maximum(x, 0.0) + jnp.log1p(jnp.exp(-jnp.abs(x)))


def _rms(x, g):
    return x * lax.rsqrt(jnp.mean(x * x, axis=-1, keepdims=True) + RMS_EPS) * g


def _dot(a, b):
    return jnp.dot(a, b, preferred_element_type=F32)


def _dot_nt(a, b):
    return lax.dot_general(a, b, (((1,), (1,)), ((), ())), preferred_element_type=F32)


def _dot_tn(a, b):
    return lax.dot_general(a, b, (((0,), (0,)), ((), ())), preferred_element_type=F32)


def _resident(shape):
    nd = len(shape)
    return pl.BlockSpec(shape, lambda *_: (0,) * nd, pipeline_mode=pl.Buffered(1))


def _norm_matmul_kernel(n_w, x_ref, g_ref, *refs):
    xn = _rms(x_ref[...], g_ref[...]).astype(BF16)
    for w_ref, o_ref in zip(refs[:n_w], refs[n_w:]):
        o_ref[...] = _dot(xn, w_ref[...])


def _norm_matmul(x2d, g, ws, tm=256):
    n, d = x2d.shape
    assert n % tm == 0
    wbytes = sum(w.size * 2 for w in ws)
    obytes = sum(2 * tm * w.shape[1] * 4 for w in ws)
    return pl.pallas_call(
        functools.partial(_norm_matmul_kernel, len(ws)),
        grid=(n // tm,),
        in_specs=[pl.BlockSpec((tm, d), lambda i: (i, 0)), _resident((1, d))] + [_resident(w.shape) for w in ws],
        out_specs=[pl.BlockSpec((tm, w.shape[1]), lambda i: (i, 0)) for w in ws],
        out_shape=[jax.ShapeDtypeStruct((n, w.shape[1]), F32) for w in ws],
        compiler_params=_cparams(("parallel",), wbytes + 2 * obytes + 6 * tm * d * 4 + (8 << 20)),
        name="norm_matmul",
    )(x2d, g.reshape(1, d), *ws)


SB_SUPER = 512
SB_SUB = 256
LOG2E = 1.4426950408889634
SB_DEAD_LOG2 = 160.0


def _sb_kernel(n_past_static, hp, q_ref, kd_ref, vd_ref, kp_ref, vp_ref, o_ref, acc_s, q_s):
    bq = q_ref.shape[0]
    dh = SB_HEAD_DIM
    q_s[...] = (q_ref[...] * (dh ** -0.5 * LOG2E)).astype(BF16)

    def neg_tri(n):
        r = lax.broadcasted_iota(jnp.int32, (n, n), 0)
        c = lax.broadcasted_iota(jnp.int32, (n, n), 1)
        return jnp.where(r > c, -1.0, 0.0).astype(BF16)

    def superblock(h, kb, vb, c, masked, first):
        cols = slice(h * dh, (h + 1) * dh)
        ks = kb.shape[0]
        sub = min(SB_SUB, ks)
        nsub = ks // sub
        z = _dot_nt(q_s[:, cols], kb.astype(BF16))
        sp = jnp.maximum(z, 0.0) + jnp.log2(1.0 + jnp.exp2(-jnp.abs(z)))
        if masked:
            allowed = lax.broadcasted_iota(jnp.int32, (bq, ks), 1) < lax.broadcasted_iota(jnp.int32, (bq, ks), 0)
            sp = jnp.where(allowed, sp, 0.0)
        sp16 = sp.astype(BF16)
        stacked = jnp.concatenate([sp16[:, k * sub:(k + 1) * sub] for k in range(nsub)], axis=0)
        after = _dot(stacked, neg_tri(sub))
        ws = [None] * nsub
        for k in reversed(range(nsub)):
            kc = slice(k * sub, (k + 1) * sub)
            w = jnp.exp2(z[:, kc] - sp[:, kc] + after[k * bq:(k + 1) * bq] + c)
            if masked:
                w = jnp.where(allowed[:, kc], w, 0.0)
            ws[k] = w.astype(BF16)
            c = c - jnp.sum(sp[:, kc], axis=1, keepdims=True)
        pv = _dot(jnp.concatenate(ws, axis=1), vb.astype(BF16))
        if first:
            acc_s[:, cols] = pv
        else:
            acc_s[:, cols] += pv
        return c

    cs = tuple(superblock(h, kd_ref[:, h * dh:(h + 1) * dh], vd_ref[:, h * dh:(h + 1) * dh],
                          jnp.zeros((bq, 1), F32), True, True) for h in range(hp))

    n_past = pl.program_id(2) if n_past_static is None else n_past_static

    def body(it, cs):
        rows = pl.ds(pl.multiple_of((n_past - 1 - it) * SB_SUPER, SB_SUPER), SB_SUPER)
        if len(kp_ref.shape) == 2:
            kbs = [kp_ref[rows, h * dh:(h + 1) * dh] for h in range(hp)]
            vbs = [vp_ref[rows, h * dh:(h + 1) * dh] for h in range(hp)]
        else:
            kbs = pltpu.einshape("thd->htd", kp_ref[rows])
            vbs = pltpu.einshape("thd->htd", vp_ref[rows])
        return tuple(superblock(h, kbs[h], vbs[h], cs[h], False, False) for h in range(hp))

    def live(cs):
        top = cs[0]
        for c in cs[1:]:
            top = jnp.maximum(top, c)
        return jnp.max(top) > -SB_DEAD_LOG2

    def w_cond(state):
        it, alive, _ = state
        return jnp.logical_and(it < n_past, alive)

    def w_body(state):
        it, _, cs = state
        cs = body(it, cs)
        return it + 1, live(cs), cs

    lax.while_loop(w_cond, w_body, (jnp.int32(0), live(cs), cs))
    o_ref[...] = acc_s[...]


def _sb_attention(q, kd, vd, kp, vp, prompt, hp):
    b, tq, width = q.shape
    heads = width // SB_HEAD_DIM
    tp = kp.shape[1]
    assert tp % SB_SUPER == 0 and heads % hp == 0
    if prompt:
        bq, n_past = SB_SUPER, None
    else:
        bq, n_past = tq, tp // SB_SUPER
    assert tq % bq == 0
    gw = hp * SB_HEAD_DIM
    qspec = pl.BlockSpec((None, bq, gw), lambda bi, h, i: (bi, i, h))
    if kp.ndim == 4:
        assert hp == heads
        pspec = pl.BlockSpec((None, tp, heads, SB_HEAD_DIM), lambda bi, h, i: (bi, 0, 0, 0))
    else:
        pspec = pl.BlockSpec((None, tp, gw), lambda bi, h, i: (bi, 0, h))
    return pl.pallas_call(
        functools.partial(_sb_kernel, n_past, hp),
        grid=(b, heads // hp, tq // bq),
        in_specs=[qspec, qspec, qspec, pspec, pspec],
        out_specs=qspec,
        out_shape=jax.ShapeDtypeStruct((b, tq, width), F32),
        scratch_shapes=[pltpu.VMEM((bq, gw), F32), pltpu.VMEM((bq, gw), BF16)],
        compiler_params=_cparams(("parallel", "parallel", "arbitrary"),
                                 4 * tp * gw * 4 + 10 * bq * gw * 4 + 12 * bq * SB_SUPER * 4 * hp + (8 << 20)),
        name="sb_attention",
    )(q, kd, vd, kp, vp)


def _rw_project_kernel(tiles_per_seq, rw_width, x_ref, g_ref, wrw_ref, sh_ref, mu_ref, w0_ref, a0_ref, wa2_ref, g2_ref,
                       r_o, w_o, k_o, v_o, a_o, g_o, last_o, carry):
    @pl.when(pl.program_id(0) % tiles_per_seq == 0)
    def _():
        carry[...] = sh_ref[...]

    xn = _rms(x_ref[...], g_ref[...]).astype(BF16)
    c2 = 2 * rw_width
    p_hi = _dot(xn, wrw_ref[:, c2:])
    p_lo = _dot(xn, wrw_ref[:, :c2])
    p = jnp.concatenate([p_lo, p_hi], axis=1)
    tm = p.shape[0]
    row = lax.broadcasted_iota(jnp.int32, p.shape, 0)
    prev = jnp.where(row == 0, carry[...], pltpu.roll(p, 1, axis=0))
    carry[...] = p[tm - 1:tm, :]
    last_o[...] = p[tm - 1:tm, :]
    ps = p + (prev - p) * mu_ref[...]
    c = rw_width
    r_o[...] = ps[:, 0:c]
    k_o[...] = ps[:, c:2 * c]
    v_o[...] = ps[:, 2 * c:3 * c]
    xwa = ps[:, 3 * c:3 * c + RW_DECAY_RANK + RW_A_RANK]
    lane = lax.broadcasted_iota(jnp.int32, xwa.shape, 1)
    lowrank_in = jnp.where(lane < RW_DECAY_RANK, jnp.tanh(xwa), xwa).astype(BF16)
    wa = _dot(lowrank_in, wa2_ref[...])
    w = -_softplus(-(w0_ref[...] + wa[:, 0:c])) - 0.5
    w_o[...] = jnp.exp(-jnp.exp(w))
    a_o[...] = _sigmoid(a0_ref[...] + wa[:, c:2 * c])
    xg = ps[:, 3 * c + RW_DECAY_RANK + RW_A_RANK:]
    g_o[...] = _dot(_sigmoid(xg).astype(BF16), g2_ref[...])


def _rw_project(x2d, norm_g, w_rw, shift0, mu, w0, a0, wa2, g2, seq_len, tm):
    n, d = x2d.shape
    proj = w_rw.shape[1]
    c = w0.shape[-1]
    assert seq_len % tm == 0 and n % seq_len == 0
    tps = seq_len // tm
    tok = pl.BlockSpec((None, tm, c), lambda i: (i // tps, i % tps, 0))
    per_seq = pl.BlockSpec((None, 1, proj), lambda i: (i // tps, 0, 0))
    return pl.pallas_call(
        functools.partial(_rw_project_kernel, tps, c),
        grid=(n // tm,),
        in_specs=[pl.BlockSpec((tm, d), lambda i: (i, 0)), _resident((1, d)), _resident(w_rw.shape), per_seq,
                  _resident((1, proj)), _resident((1, c)), _resident((1, c)), _resident(wa2.shape), _resident(g2.shape)],
        out_specs=[tok] * 6 + [per_seq],
        out_shape=[jax.ShapeDtypeStruct((n // seq_len, seq_len, c), F32)] * 6
        + [jax.ShapeDtypeStruct((n // seq_len, 1, proj), F32)],
        scratch_shapes=[pltpu.VMEM((1, proj), F32)],
        compiler_params=_cparams(("arbitrary",), w_rw.size * 2 + 6 * tm * proj * 4 + 14 * tm * c * 4 + 4 * tm * d * 4 + (8 << 20)),
        name="rw_project",
    )(x2d, norm_g.reshape(1, d), w_rw, shift0, mu.reshape(1, proj), w0.reshape(1, c), a0.reshape(1, c), wa2, g2)


def _rw_scan_kernel(r_ref, w_ref, k_ref, v_ref, a_ref, kk_p, ka_p, rk_p, lng_p, lnb_p, s0_ref,
                    o_ref, S, kk_s, b_s, kt_s, r_s, bonus_s, g_s):
    j = pl.program_id(1)
    nk = S.shape[0]
    tt = r_ref.shape[0]

    @pl.when(j == 0)
    def _():
        S[...] = s0_ref[...]

    def prep(t, g):
        r = r_ref[t]
        k = k_ref[t]
        a = a_ref[t]
        kk = k * kk_p[...]
        nrm = jnp.sqrt(jnp.sum(kk * kk, axis=0, keepdims=True))
        kk = kk / jnp.maximum(nrm, 1e-12)
        kt = k * (1.0 + (a - 1.0) * ka_p[...])
        g_new = g * w_ref[t]
        inv_g = 1.0 / g_new
        kk_s[t] = kk * g
        b_s[t] = (kk * a) * inv_g
        kt_s[t] = kt * inv_g
        r_s[t] = r * g_new
        bonus_s[t] = jnp.sum(r * kt * rk_p[...], axis=0, keepdims=True) * v_ref[t]
        return g_new

    g_s[...] = lax.fori_loop(0, tt, prep, jnp.ones(g_s.shape, F32), unroll=2)

    def step(t, carry):
        v = v_ref[t]

        def p1(kx, sa):
            return sa + S[kx] * kk_s[t, pl.ds(kx, 1), :]

        sa = lax.fori_loop(0, nk, p1, jnp.zeros(v.shape, F32), unroll=16)

        def p2(kx, y):
            row = pl.ds(kx, 1)
            s_new = S[kx] - sa * b_s[t, row, :] + v * kt_s[t, row, :]
            S[kx] = s_new
            return y + s_new * r_s[t, row, :]

        o_ref[t] = lax.fori_loop(0, nk, p2, jnp.zeros(v.shape, F32), unroll=16)
        return carry

    lax.fori_loop(0, tt, step, 0)

    def post(t, carry):
        y = o_ref[t]
        mean = jnp.mean(y, axis=0, keepdims=True)
        d = y - mean
        var = jnp.mean(d * d, axis=0, keepdims=True)
        o_ref[t] = d * lax.rsqrt(var + GN_EPS) * lng_p[...] + lnb_p[...] + bonus_s[t]
        return carry

    lax.fori_loop(0, tt, post, 0, unroll=2)

    def fold(kx, carry):
        S[kx] = S[kx] * g_s[pl.ds(kx, 1), :]
        return carry

    lax.fori_loop(0, nk, fold, 0, unroll=16)


def _rw_scan(rT, wT, kT, vT, aT, params, s0, tt):
    t, n, ctot = rT.shape
    assert t % tt == 0 and ctot % V7X_LANES == 0
    seq = pl.BlockSpec((tt, n, V7X_LANES), lambda c, j: (j, 0, c))
    par = pl.BlockSpec((n, V7X_LANES), lambda c, j: (0, c))
    st = pl.BlockSpec((n, n, V7X_LANES), lambda c, j: (0, 0, c))
    return pl.pallas_call(
        _rw_scan_kernel,
        grid=(ctot // V7X_LANES, t // tt),
        in_specs=[seq] * 5 + [par] * 5 + [st],
        out_specs=[seq, st],
        out_shape=[jax.ShapeDtypeStruct((t, n, ctot), F32), jax.ShapeDtypeStruct((n, n, ctot), F32)],
        scratch_shapes=[pltpu.VMEM((tt, n, V7X_LANES), F32)] * 5 + [pltpu.VMEM((n, V7X_LANES), F32)],
        compiler_params=_cparams(("parallel", "arbitrary"), 17 * tt * n * V7X_LANES * 4 + 5 * n * n * V7X_LANES * 4 + (8 << 20)),
        name="rw_scan",
    )(rT, wT, kT, vT, aT, *params, s0)


def _mix_kernel(osb_ref, orw_ref, g_ref, gsb_ref, grw_ref, x_ref, wsb_ref, wrw_ref, wout_ref, h_ref):
    sb = _dot(osb_ref[...].astype(BF16), wsb_ref[...])
    rw = _dot((orw_ref[...] * g_ref[...]).astype(BF16), wrw_ref[...])
    mixed = _sigmoid(gsb_ref[...]) * sb + _sigmoid(grw_ref[...]) * rw
    h_ref[...] = x_ref[...] + _dot(mixed.astype(BF16), wout_ref[...])


def _mix(o_sb, o_rw, g, gate_sb, gate_rw, x2d, wsb, wrw, wout, tm=256):
    n, d = x2d.shape
    c = o_sb.shape[1]
    assert n % tm == 0
    half = pl.BlockSpec((tm, c), lambda i: (i, 0))
    full = pl.BlockSpec((tm, d), lambda i: (i, 0))
    return pl.pallas_call(
        _mix_kernel,
        grid=(n // tm,),
        in_specs=[half, half, half, full, full, full, _resident(wsb.shape), _resident(wrw.shape), _resident(wout.shape)],
        out_specs=full,
        out_shape=jax.ShapeDtypeStruct((n, d), F32),
        compiler_params=_cparams(("parallel",), (wsb.size + wrw.size + wout.size) * 2 + 2 * tm * (3 * c + 4 * d) * 4 + (12 << 20)),
        name="branch_mix",
    )(o_sb, o_rw, g, gate_sb, gate_rw, x2d, wsb, wrw, wout)


PEER_RANKS = PEER_TOPK
PEER_RANK_ROWS = 16
PEER_UNRANKED = 127.0


def _top_ranks(s, t_ref, want_ranks):
    t_ref[...] = jnp.full(t_ref.shape, NEG_BIG, F32)
    ranks = jnp.full(s.shape, PEER_UNRANKED, F32) if want_ranks else None
    for rnk in range(PEER_RANKS):
        m = jnp.max(s, axis=0, keepdims=True)
        t_ref[rnk:rnk + 1, :] = m
        hit = s == m
        if want_ranks:
            ranks = jnp.where(hit, float(rnk), ranks)
        s = jnp.where(hit, NEG_BIG, s)
    return ranks


def _peer_select_kernel(h_ref, g_ref, wq_ref, k1_ref, k2_ref, xn_o, lim_o, coef_o, rank2_o, e2_o, t1_s, t2_s):
    xn = _rms(h_ref[...], g_ref[...]).astype(BF16)
    xn_o[...] = xn
    q = _dot(xn, wq_ref[...])
    k1 = k1_ref[...].astype(BF16)
    k2 = k2_ref[...].astype(BF16)
    sub = V7X_SUBLANES

    for h in range(PEER_HEADS):
        base = 2 * PEER_HALF * h
        s1 = _dot_nt(k1, q[:, base:base + PEER_HALF].astype(BF16))
        s2 = _dot_nt(k2, q[:, base + PEER_HALF:base + 2 * PEER_HALF].astype(BF16))
        _top_ranks(s1, t1_s, False)
        rank2 = _top_ranks(s2, t2_s, True)
        t1 = t1_s[...]
        t2 = t2_s[...]
        cand = jnp.concatenate(
            [t1[0:1] + t2[0:sub], t1[0:1] + t2[sub:2 * sub],
             t2[0:1] + t1[0:sub], t2[0:1] + t1[sub:2 * sub],
             t1[1:2] + t2[0:sub], t2[1:2] + t1[0:sub],
             t1[2:3] + t2[0:sub], t1[3:4] + t2[0:sub], t1[4:5] + t2[0:sub]], axis=0)
        top = jnp.max(cand, axis=0, keepdims=True)
        zsum = jnp.zeros(top.shape, F32)
        for rnk in range(PEER_TOPK):
            tau = jnp.max(cand, axis=0, keepdims=True)
            zsum = zsum + jnp.exp(tau - top)
            cand = jnp.where(cand == tau, NEG_BIG, cand)
        lim = jnp.zeros(s1.shape, F32)
        for a in range(PEER_TOPK):
            width = jnp.sum(jnp.where(t1[a:a + 1] + t2 >= tau, 1.0, 0.0), axis=0, keepdims=True)
            lim = jnp.where(s1 == t1[a:a + 1], width, lim)
        lim_o[h] = lim
        coef_o[h] = jnp.exp(s1 - t1[0:1]) / zsum
        rank2_o[h] = rank2.astype(BF16)
        e2_o[h] = jnp.exp(s2 - t2[0:1]).astype(BF16)


def _peer_select(h2d, g, wq, k1, k2, tn):
    n, d = h2d.shape
    assert n % tn == 0
    sel = pl.BlockSpec((PEER_HEADS, PEER_NKEYS, tn), lambda i: (0, 0, i))
    sel_f32 = jax.ShapeDtypeStruct((PEER_HEADS, PEER_NKEYS, n), F32)
    sel_bf16 = jax.ShapeDtypeStruct((PEER_HEADS, PEER_NKEYS, n), BF16)
    return pl.pallas_call(
        _peer_select_kernel,
        grid=(n // tn,),
        in_specs=[pl.BlockSpec((tn, d), lambda i: (i, 0)), _resident((1, d)), _resident(wq.shape),
                  _resident(k1.shape), _resident(k2.shape)],
        out_specs=[pl.BlockSpec((tn, d), lambda i: (i, 0)), sel, sel, sel, sel],
        out_shape=[jax.ShapeDtypeStruct((n, d), BF16), sel_f32, sel_f32, sel_bf16, sel_bf16],
        scratch_shapes=[pltpu.VMEM((PEER_RANK_ROWS, tn), F32)] * 2,
        compiler_params=_cparams(("parallel",), wq.size * 2 + 8 * PEER_HEADS * PEER_NKEYS * tn * 4 + 8 * tn * d * 4 + (12 << 20)),
        name="peer_select",
    )(h2d, g.reshape(1, d), wq, k1, k2)


def _peer_main_kernel(xn_ref, u_ref, v_ref, lim_ref, coef_ref, rank2_ref, e2_ref, h_ref, g_ref, y_ref, acc):
    j = pl.program_id(1)
    eb = u_ref.shape[0]
    nib = eb // PEER_NKEYS
    tile = (PEER_NKEYS, xn_ref.shape[0])

    @pl.when(j == 0)
    def _():
        acc[...] = jnp.zeros(acc.shape, F32)

    hid = _dot_nt(u_ref[...], xn_ref[...])
    parts = []
    for ii in range(nib):
        row = pl.ds(j * nib + ii, 1)
        wgt = jnp.zeros(tile, BF16)
        for h in range(PEER_HEADS):
            lim = jnp.broadcast_to(lim_ref[h, row, :], tile).astype(BF16)
            coef = jnp.broadcast_to(coef_ref[h, row, :], tile).astype(BF16)
            picked = jnp.where(rank2_ref[h] < lim, e2_ref[h], jnp.zeros((), BF16))
            wgt = wgt + coef * picked
        x = hid[ii * PEER_NKEYS:(ii + 1) * PEER_NKEYS, :].astype(BF16)
        gelu = (0.5 * x) * (1.0 + lax.erf(x * (2.0 ** -0.5)))
        parts.append(gelu * wgt)
    gated = jnp.concatenate(parts, axis=0)
    acc[...] += _dot_tn(gated, v_ref[...])

    @pl.when(j == pl.num_programs(1) - 1)
    def _():
        y_ref[...] = _rms(h_ref[...] + acc[...], g_ref[...])


def _peer_main(xn, u, v, lim, coef, rank2, e2, h2d, g, tn, eb):
    n, d = h2d.shape
    e = u.shape[0]
    assert n % tn == 0 and e % eb == 0 and eb % PEER_NKEYS == 0
    sel = pl.BlockSpec((PEER_HEADS, PEER_NKEYS, tn), lambda i, j: (0, 0, i))
    tok = pl.BlockSpec((tn, d), lambda i, j: (i, 0))
    tab = pl.BlockSpec((eb, d), lambda i, j: (j, 0))
    vmem = 2 * tn * d * 2 + 4 * eb * d * 2 + 6 * PEER_HEADS * PEER_NKEYS * tn * 4 + 5 * tn * d * 4 + (8 << 20)
    return pl.pallas_call(
        _peer_main_kernel,
        grid=(n // tn, e // eb),
        in_specs=[tok, tab, tab, sel, sel, sel, sel, tok, _resident((1, d))],
        out_specs=tok,
        out_shape=jax.ShapeDtypeStruct((n, d), F32),
        scratch_shapes=[pltpu.VMEM((tn, d), F32)],
        compiler_params=_cparams(("parallel", "arbitrary"), vmem),
        name="peer_main",
    )(xn, u, v, lim, coef, rank2, e2, h2d, g.reshape(1, d))


def _to_chain_major(x, heads):
    b, t, c = x.shape
    n = c // heads
    return x.reshape(b, t, heads, n).transpose(1, 3, 0, 2).reshape(t, n, b * heads)


def _from_chain_major(x, b):
    t, n, ch = x.shape
    heads = ch // b
    return x.reshape(t, n, b, heads).transpose(2, 0, 3, 1).reshape(b, t, heads * n)


def _head_param(p, heads, b):
    n = p.size // heads
    return jnp.tile(p.reshape(heads, n).T, (1, b))


def _layer(x, past_k, past_v, wkv0, shift0, wts, sb_hp, rw_tt, peer_tn, peer_eb):
    b, t, d = x.shape
    n_tok = b * t
    x2d = x.reshape(n_tok, d)
    sbw = wts["wq"].shape[1]
    rww = wts["w0"].shape[-1]
    rw_heads = rww // RW_HEAD_DIM
    proj = wts["w_rw"].shape[1]

    q, k, v, gate_sb, gate_rw = _norm_matmul(
        x2d, wts["norm1_g"], [wts["wq"], wts["wk"], wts["wv"], wts["w_gsb"], wts["w_grw"]])

    q3, k3, v3 = (a.reshape(b, t, sbw) for a in (q, k, v))
    heads_sb = sbw // SB_HEAD_DIM
    if past_k is None:
        o_sb = _sb_attention(q3, k3, v3, k3, v3, prompt=True, hp=sb_hp)
        wkv0 = jnp.zeros((b, rw_heads, RW_HEAD_DIM, RW_HEAD_DIM), F32)
        shift0 = jnp.zeros((b, 1, proj), F32)
    else:
        o_sb = _sb_attention(q3, k3, v3, past_k, past_v, prompt=False, hp=sb_hp)

    r, w, kr, vr, a, g, shift_new = _rw_project(x2d, wts["norm1_g"], wts["w_rw"], shift0, wts["rw_mu"], wts["w0"],
                                                wts["a0"], wts["wa2"], wts["g2"], seq_len=t, tm=min(t, 256))
    seqs = [_to_chain_major(s, rw_heads) for s in (r, w, kr, vr, a)]
    params = [_head_param(wts[nm], rw_heads, b) for nm in ("k_k", "k_a", "r_k", "ln_g", "ln_b")]
    s0 = wkv0.astype(F32).transpose(3, 2, 0, 1).reshape(RW_HEAD_DIM, RW_HEAD_DIM, b * rw_heads)
    oT, sT = _rw_scan(*seqs, params, s0, tt=rw_tt)
    o_rw = _from_chain_major(oT, b).reshape(n_tok, rww)
    wkv_new = sT.reshape(RW_HEAD_DIM, RW_HEAD_DIM, b, rw_heads).transpose(2, 3, 1, 0)

    h = _mix(o_sb.reshape(n_tok, sbw), o_rw, g.reshape(n_tok, rww), gate_sb, gate_rw, x2d,
             wts["w_bsb"], wts["w_brw"], wts["w_out"])
    xn2, lim, coef, rank2, e2 = _peer_select(h, wts["norm2_g"], wts["peer_wq"], wts["peer_k1"], wts["peer_k2"], tn=256)
    y = _peer_main(xn2, wts["peer_u"], wts["peer_v"], lim, coef, rank2, e2, h, wts["final_g"], tn=peer_tn, eb=peer_eb)
    kv_shape = (b, t, heads_sb, SB_HEAD_DIM)
    return y.reshape(b, t, d), k.reshape(kv_shape), v.reshape(kv_shape), wkv_new, shift_new


def kernel(x_prompt, x_sample, cache_sb_k, cache_sb_v, state_rwkv_wkv, state_rwkv_shift, norm1_g, w_in, rw_mu, rw_w0, rw_w2, rw_a0, rw_a2, rw_g2, rw_k_k, rw_k_a, rw_r_k, rw_ln_g, rw_ln_b, w_branch_sb, w_branch_rw, w_out, norm2_g, peer_wq, peer_k1, peer_k2, peer_u, peer_v, final_norm_g):
    assert norm1_g.shape[0] == 1, "single-layer stack"
    d = x_prompt.shape[-1]
    sbw = w_branch_sb.shape[1]
    rww = w_branch_rw.shape[1]
    proj = rw_mu.shape[1]
    w = w_in[0]
    o = 0
    cols = {}
    for nm, width in (("wq", sbw), ("wk", sbw), ("wv", sbw), ("w_rw", proj), ("w_gsb", d), ("w_grw", d)):
        cols[nm] = w[:, o:o + width].astype(BF16)
        o += width
    zeros = jnp.zeros((RW_DECAY_RANK, rww), F32)
    wa2 = jnp.concatenate([jnp.concatenate([rw_w2[0], zeros], axis=1),
                           jnp.concatenate([zeros, rw_a2[0]], axis=1)], axis=0).astype(BF16)
    wts = dict(cols, norm1_g=norm1_g[0], rw_mu=rw_mu[0], w0=rw_w0[0], a0=rw_a0[0], wa2=wa2, g2=rw_g2[0].astype(BF16),
               k_k=rw_k_k[0], k_a=rw_k_a[0], r_k=rw_r_k[0].reshape(-1), ln_g=rw_ln_g[0], ln_b=rw_ln_b[0],
               w_bsb=w_branch_sb[0].astype(BF16), w_brw=w_branch_rw[0].astype(BF16), w_out=w_out[0].astype(BF16),
               norm2_g=norm2_g[0], peer_wq=peer_wq[0].astype(BF16), peer_k1=peer_k1[0], peer_k2=peer_k2[0],
               peer_u=peer_u[0].astype(BF16), peer_v=peer_v[0].astype(BF16), final_g=final_norm_g)

    sb_heads = sbw // SB_HEAD_DIM
    yp, kp, vp, wp, sp = _layer(x_prompt, None, None, None, None, wts, sb_hp=4, rw_tt=32, peer_tn=512, peer_eb=1024)
    ys, ks, vs, ws, ss = _layer(x_sample, cache_sb_k[0], cache_sb_v[0], state_rwkv_wkv[0], state_rwkv_shift[0], wts,
                                sb_hp=sb_heads, rw_tt=32, peer_tn=512, peer_eb=1024)
    st = lambda t: t[None]
    return (yp, ys, st(kp), st(vp), st(wp.astype(x_prompt.dtype)), st(sp), st(ks), st(vs), st(ws.astype(x_sample.dtype)), st(ss))
```
